```python
import jax, jax.numpy as jnp
from jax import lax
import numpy as np

D_MODEL = 1024
BATCH = 8
SEQ = 8192
DEPTH = 2

D_PLE = 256
CHUNK = 64

SSD_HEADS = 8
SSD_HEAD_DIM = 64
SSD_WIDTH = SSD_HEADS * SSD_HEAD_DIM
SSD_GROUPS = 2
SSD_STATE = 128
SSD_CONV = 5
SSD_CONV_CH = SSD_WIDTH + 2 * SSD_GROUPS * SSD_STATE

GLA_HEADS = 4
GLA_DK = 32
GLA_DV = 64
GLA_WIDTH = GLA_HEADS * GLA_DV
GLA_GATE_RANK = 16
GLA_GATE_TEMP = 16.0

RET_HEADS = 4
RET_DK = 64
RET_DV = 64
RET_WIDTH = RET_HEADS * RET_DV
ROPE_BASE = 10000.0

D_MIX = SSD_WIDTH + GLA_WIDTH + RET_WIDTH

SPLIT_SIZES = [
    SSD_WIDTH,
    SSD_CONV_CH,
    2 * SSD_HEADS,
    GLA_HEADS * GLA_DK,
    GLA_HEADS * GLA_DK,
    GLA_WIDTH,
    GLA_WIDTH,
    2 * GLA_GATE_RANK,
    RET_HEADS * RET_DK,
    RET_HEADS * RET_DK,
    RET_WIDTH,
    RET_WIDTH,
]
N_IN = int(sum(SPLIT_SIZES))
SPLIT_IDX = [int(i) for i in np.cumsum(SPLIT_SIZES)[:-1]]

DN_ALPHA = float((2 * DEPTH) ** 0.25)
DN_BETA = float((8 * DEPTH) ** -0.25)
LN_EPS = 1e-5
RMS_EPS = 1e-6

kernel_name = "bidir_hybrid_ssd_gla_retention_deepnorm"


def layer_norm(x, w, b):
    xf = x.astype(jnp.float32)
    mu = jnp.mean(xf, axis=-1, keepdims=True)
    var = jnp.mean(jnp.square(xf - mu), axis=-1, keepdims=True)
    return ((xf - mu) * lax.rsqrt(var + LN_EPS) * w + b).astype(x.dtype)


def rms_norm(x, w):
    xf = x.astype(jnp.float32)
    return (xf * lax.rsqrt(jnp.mean(jnp.square(xf), axis=-1, keepdims=True) + RMS_EPS) * w).astype(x.dtype)


def centred_depthwise_conv(x, w, b):
    k = w.shape[0]
    y = lax.conv_general_dilated(
        x, w[:, None, :].astype(x.dtype), window_strides=(1,),
        padding=[(k // 2, k // 2)], dimension_numbers=("NWC", "WIO", "NWC"),
        feature_group_count=x.shape[-1])
    return y + b


def _to_chunks(t):
    return t.astype(jnp.float32).reshape(t.shape[0], t.shape[1] // CHUNK, CHUNK, *t.shape[2:])


def _carry_states(chunk_states, chunk_decay):
    def step(carry, inp):
        s, d = inp
        return carry * d[..., None] + s, carry
    init = jnp.zeros_like(chunk_states[:, 0])
    _, entering = lax.scan(step, init, (jnp.moveaxis(chunk_states, 1, 0), jnp.moveaxis(chunk_decay, 1, 0)))
    return jnp.moveaxis(entering, 0, 1)


def _mask(inclusive):
    return jnp.tril(jnp.ones((CHUNK, CHUNK), dtype=bool), k=0 if inclusive else -1)


def scalar_decay_scan(q, k, v, log_a, inclusive):
    b, s_len, h, _ = q.shape
    q, k, v, la = _to_chunks(q), _to_chunks(k), _to_chunks(v), _to_chunks(log_a)
    cum = jnp.cumsum(la, axis=2)
    seg = cum[:, :, :, None, :] - cum[:, :, None, :, :]
    m = _mask(inclusive)[None, None, :, :, None]
    decay = jnp.exp(jnp.where(m, seg, -jnp.inf))
    scores = jnp.einsum("bnthk,bnshk->bntsh", q, k) * decay
    y = jnp.einsum("bntsh,bnshv->bnthv", scores, v)
    last = cum[:, :, -1]
    w_state = jnp.exp(last[:, :, None, :] - cum)
    states = jnp.einsum("bnsh,bnshk,bnshv->bnhkv", w_state, k, v)
    entering = _carry_states(states, jnp.exp(last)[..., None])
    y = y + jnp.einsum("bnthk,bnhkv->bnthv", q * jnp.exp(cum)[..., None], entering)
    return y.reshape(b, s_len, h, v.shape[-1])


def vector_decay_scan(q, k, v, log_a, inclusive):
    b, s_len, h, _ = q.shape
    q, k, v, la = _to_chunks(q), _to_chunks(k), _to_chunks(v), _to_chunks(log_a)
    cum = jnp.cumsum(la, axis=2)
    q_dec = q * jnp.exp(cum)
    scores = jnp.einsum("bnthk,bnshk->bntsh", q_dec, k * jnp.exp(-cum))
    scores = jnp.where(_mask(inclusive)[None, None, :, :, None], scores, 0.0)
    y = jnp.einsum("bntsh,bnshv->bnthv", scores, v)
    last = cum[:, :, -1]
    states = jnp.einsum("bnshk,bnshv->bnhkv", k * jnp.exp(last[:, :, None] - cum), v)
    entering = _carry_states(states, jnp.exp(last))
    y = y + jnp.einsum("bnthk,bnhkv->bnthv", q_dec, entering)
    return y.reshape(b, s_len, h, v.shape[-1])


def bidirectional(scan_fn, q, k, v_f, v_b, la_f, la_b):
    flip = lambda t: jnp.flip(t, axis=1)
    fwd = scan_fn(q, k, v_f, la_f, True)
    bwd = flip(scan_fn(flip(q), flip(k), flip(v_b), flip(la_b), False))
    return fwd + bwd


def rotary(t):
    s_len, d = t.shape[1], t.shape[-1]
    half = d // 2
    inv = ROPE_BASE ** (-jnp.arange(half, dtype=jnp.float32) / half)
    ang = jnp.arange(s_len, dtype=jnp.float32)[:, None] * inv[None, :]
    cos, sin = jnp.cos(ang)[None, :, None, :], jnp.sin(ang)[None, :, None, :]
    t1, t2 = t[..., :half].astype(jnp.float32), t[..., half:].astype(jnp.float32)
    return jnp.concatenate([t1 * cos - t2 * sin, t1 * sin + t2 * cos], axis=-1)


def ssd_branch(z, xbc, dt_raw, conv_w, conv_b, dt_bias, a_log, d_skip, norm_w):
    b, s_len, _ = z.shape
    xbc = jax.nn.silu(centred_depthwise_conv(xbc, conv_w, conv_b))
    xs, bm, cm = jnp.split(xbc, [SSD_WIDTH, SSD_WIDTH + SSD_GROUPS * SSD_STATE], axis=-1)
    xs = xs.reshape(b, s_len, SSD_HEADS, SSD_HEAD_DIM)
    rep = SSD_HEADS // SSD_GROUPS
    bh = jnp.repeat(bm.reshape(b, s_len, SSD_GROUPS, SSD_STATE), rep, axis=2)
    ch = jnp.repeat(cm.reshape(b, s_len, SSD_GROUPS, SSD_STATE), rep, axis=2)
    dt = jax.nn.softplus(dt_raw.astype(jnp.float32).reshape(b, s_len, 2, SSD_HEADS) + dt_bias)
    la = dt * (-jnp.exp(a_log))
    xf = xs.astype(jnp.float32)
    y = bidirectional(scalar_decay_scan, ch, bh,
                      xf * dt[:, :, 0, :, None], xf * dt[:, :, 1, :, None],
                      la[:, :, 0], la[:, :, 1])
    y = (y + d_skip[:, None] * xf).reshape(b, s_len, SSD_WIDTH)
    return rms_norm(y * jax.nn.silu(z.astype(jnp.float32)), norm_w)


def gla_branch(q, k, v, g, a_lr, w_a2, b_a, norm_w):
    b, s_len, _ = q.shape
    q = q.reshape(b, s_len, GLA_HEADS, GLA_DK) * (GLA_DK ** -0.5)
    k = k.reshape(b, s_len, GLA_HEADS, GLA_DK)
    v = v.reshape(b, s_len, GLA_HEADS, GLA_DV)
    a_lr = a_lr.astype(jnp.float32).reshape(b, s_len, 2, GLA_GATE_RANK)
    la = jax.nn.log_sigmoid(jnp.einsum("bsdr,drk->bsdk", a_lr, w_a2) + b_a) / GLA_GATE_TEMP
    la = la.reshape(b, s_len, 2, GLA_HEADS, GLA_DK)
    o = bidirectional(vector_decay_scan, q, k, v, v, la[:, :, 0], la[:, :, 1])
    o = rms_norm(o, norm_w).reshape(b, s_len, GLA_WIDTH)
    return o * jax.nn.silu(g.astype(jnp.float32))


def retention_branch(q, k, v, g, norm_w, norm_b):
    b, s_len, _ = q.shape
    q = rotary(q.reshape(b, s_len, RET_HEADS, RET_DK))
    k = rotary(k.reshape(b, s_len, RET_HEADS, RET_DK)) * (RET_DK ** -0.5)
    v = v.reshape(b, s_len, RET_HEADS, RET_DV)
    log_gamma = jnp.log(1.0 - 2.0 ** (-5.0 - jnp.arange(RET_HEADS, dtype=jnp.float32)))
    la = jnp.broadcast_to(log_gamma, (b, s_len, RET_HEADS))
    o = bidirectional(scalar_decay_scan, q, k, v, v, la, la)
    o = layer_norm(o, norm_w.reshape(RET_HEADS, RET_DV), norm_b.reshape(RET_HEADS, RET_DV))
    return o.reshape(b, s_len, RET_WIDTH) * jax.nn.silu(g.astype(jnp.float32))


def setup_inputs(seed: int = 0) -> dict:
    key = jax.random.key(seed)
    ks = jax.random.split(key, 24)
    f32 = jnp.float32
    nrm = lambda k, shape, scale: jax.random.normal(k, shape, f32) * scale
    dt0 = jnp.exp(jax.random.uniform(ks[5], (DEPTH, 2, SSD_HEADS), f32, np.log(1e-3), np.log(1e-1)))
    return {
        "x": nrm(ks[0], (BATCH, SEQ, D_MODEL), 1.0),
        "p": nrm(ks[1], (DEPTH, BATCH, SEQ, D_PLE), 1.0),
        "w_in": nrm(ks[2], (DEPTH, D_MODEL, N_IN), D_MODEL ** -0.5),
        "conv_w": nrm(ks[3], (DEPTH, SSD_CONV, SSD_CONV_CH), SSD_CONV ** -0.5),
        "conv_b": nrm(ks[4], (DEPTH, SSD_CONV_CH), 0.02),
        "dt_bias": dt0 + jnp.log(-jnp.expm1(-dt0)),
        "a_log": jnp.log(jax.random.uniform(ks[6], (DEPTH, 2, SSD_HEADS), f32, 1.0, 16.0)),
        "d_skip": 1.0 + nrm(ks[7], (DEPTH, SSD_HEADS), 0.1),
        "ssd_norm_w": 1.0 + nrm(ks[8], (DEPTH, SSD_WIDTH), 0.02),
        "gla_w_a2": nrm(ks[9], (DEPTH, 2, GLA_GATE_RANK, GLA_HEADS * GLA_DK), GLA_GATE_RANK ** -0.5),
        "gla_b_a": nrm(ks[10], (DEPTH, 2, GLA_HEADS * GLA_DK), 0.1),
        "gla_norm_w": 1.0 + nrm(ks[11], (DEPTH, GLA_DV), 0.02),
        "ret_norm_w": 1.0 + nrm(ks[12], (DEPTH, RET_WIDTH), 0.02),
        "ret_norm_b": nrm(ks[13], (DEPTH, RET_WIDTH), 0.02),
        "w_out": nrm(ks[14], (DEPTH, D_MIX, D_MODEL), DN_BETA * D_MIX ** -0.5),
        "ln_w": 1.0 + nrm(ks[15], (DEPTH, D_MODEL), 0.02),
        "ln_b": nrm(ks[16], (DEPTH, D_MODEL), 0.02),
        "w_pe": nrm(ks[17], (DEPTH, D_PLE, D_MODEL), D_PLE ** -0.5),
        "w_pg": nrm(ks[18], (DEPTH, D_MODEL, D_MODEL), D_MODEL ** -0.5),
        "b_pg": nrm(ks[19], (DEPTH, D_MODEL), 0.02),
    }


def reference(x, p, w_in, conv_w, conv_b, dt_bias, a_log, d_skip, ssd_norm_w,
              gla_w_a2, gla_b_a, gla_norm_w, ret_norm_w, ret_norm_b, w_out,
              ln_w, ln_b, w_pe, w_pg, b_pg):
    h = x
    for i in range(DEPTH):
        u = jnp.einsum("bsd,dn->bsn", h, w_in[i])
        (z, xbc, dt_raw, gq, gk, gv, gg, ga, rq, rk, rv, rg) = jnp.split(u, SPLIT_IDX, axis=-1)
        y_ssd = ssd_branch(z, xbc, dt_raw, conv_w[i], conv_b[i], dt_bias[i], a_log[i],
                           d_skip[i], ssd_norm_w[i])
        y_gla = gla_branch(gq, gk, gv, gg, ga, gla_w_a2[i], gla_b_a[i], gla_norm_w[i])
        y_ret = retention_branch(rq, rk, rv, rg, ret_norm_w[i], ret_norm_b[i])
        y_cat = jnp.concatenate([y_ssd, y_gla, y_ret], axis=-1).astype(h.dtype)
        mix = jnp.einsum("bsm,md->bsd", y_cat, w_out[i])
        h = layer_norm(DN_ALPHA * h + mix, ln_w[i], ln_b[i])
        gate = jax.nn.sigmoid(jnp.einsum("bsd,de->bse", h, w_pg[i]) + b_pg[i])
        h = h + gate * jnp.einsum("bsp,pd->bsd", p[i], w_pe[i])
    return h
```

```python
import functools

import jax
import jax.numpy as jnp
from jax import lax
from jax.experimental import pallas as pl
from jax.experimental.pallas import tpu as pltpu

F32 = jnp.float32
BF16 = jnp.bfloat16

D_MODEL = 1024
D_PLE = 256
DEPTH = 2
SSD_HEADS = 8
SSD_WIDTH = 512
SSD_STATE = 128
SSD_CONV = 5
SSD_CONV_CH = 1024
GLA_DK = 32
GLA_TEMP = 16.0
RET_DK = 64
ROPE_BASE = 10000.0
DN_ALPHA = float((2 * DEPTH) ** 0.25)
LN_EPS = 1e-5
RMS_EPS = 1e-6

_Z0, _XBC0, _DT0, _GQ0, _GK0, _GV0, _GG0, _GA0, _RQ0, _RK0, _RV0, _RG0, _NIN = (
    0, 512, 1536, 1552, 1680, 1808, 2064, 2320, 2352, 2608, 2864, 3120, 3376)
N_PROJ = 3456
REST0 = 1536
N_ACT = 3328
N_DEC = 512

TILE = 256
HALO = 16
GLA_CHUNK = 64
VMEM_LIMIT = 56 * 1024 * 1024


def _mm(a, b):
    return jnp.dot(a, b, preferred_element_type=F32)


def _mm_nt(a, b):
    return lax.dot_general(a, b, (((1,), (1,)), ((), ())), preferred_element_type=F32)


def _mm_tn(a, b):
    return lax.dot_general(a, b, (((0,), (0,)), ((), ())), preferred_element_type=F32)


def _iota(shape, dim):
    return lax.broadcasted_iota(jnp.int32, shape, dim)


def _split(x, parts):
    out, r = [], x
    for k in range(parts):
        piece = r.astype(BF16)
        out.append(piece)
        if k + 1 < parts:
            r = r - piece.astype(F32)
    return jnp.concatenate(out, axis=1)


def _fold(y, parts):
    w = y.shape[1] // parts
    acc = y[:, :w]
    for k in range(1, parts):
        acc = acc + y[:, k * w:(k + 1) * w]
    return acc


def _cumsum_rows(tri, x):
    return _fold(_mm(tri, _split(x, 3)), 3)


def _head_expander(first_lane, parts):
    r = _iota((128 * parts, SSD_WIDTH), 0) & 127
    c = _iota((128 * parts, SSD_WIDTH), 1) >> 6
    return jnp.where(r == c + first_lane, 1.0, 0.0).astype(BF16)


def _expand(blk, expander):
    return _mm(_split(blk, 2), expander)


def _sigmoid(x):
    return 1.0 / (1.0 + jnp.exp(-x))


def _softplus(x):
    return jnp.maximum(x, 0.0) + jnp.log1p(jnp.exp(-jnp.abs(x)))


def _tile_rows(x, n):
    return jnp.concatenate([x] * n, axis=0)


def _rope_body(inv_ref, cos_ref, sin_ref):
    rows = cos_ref.shape[0]
    pos = (_iota((rows, 256), 0) + pl.program_id(0) * rows).astype(F32)
    ang = pos * inv_ref[...]
    first_half = (_iota((rows, 256), 1) & 63) < 32
    s = jnp.sin(ang)
    cos_ref[...] = jnp.cos(ang)
    sin_ref[...] = jnp.where(first_half, -s, s)


def _rope_table(inv_row, seq):
    rows = 512
    return pl.pallas_call(
        _rope_body,
        grid=(seq // rows,),
        in_specs=[pl.BlockSpec((1, 256), lambda i: (0, 0))],
        out_specs=[pl.BlockSpec((rows, 256), lambda i: (i, 0))] * 2,
        out_shape=[jax.ShapeDtypeStruct((seq, 256), F32)] * 2,
        name="rope_table",
    )(inv_row)


def _prep_body(hp_ref, h_ref, hn_ref, w_ref, cw_ref, cb_ref, dtb_ref, alog_ref, wsm_ref, ba_ref,
               cos_ref, sin_ref, lg_ref,
               acts_ref, dec_ref, sbs_ref, sbg_ref, sbr_ref,
               xs_ref, st_s, st_g, st_r):
    T = TILE
    i = pl.program_id(1)
    n = pl.num_programs(1)
    t = n - 1 - i

    @pl.when(i == 0)
    def _():
        st_s[...] = jnp.zeros_like(st_s)
        st_g[...] = jnp.zeros_like(st_g)
        st_r[...] = jnp.zeros_like(st_r)

    hp = jnp.where(t > 0, hp_ref[...], 0.0)
    hn = jnp.where(t < n - 1, hn_ref[...], 0.0)
    hext = jnp.concatenate([hp, h_ref[...], hn], axis=0).astype(BF16)
    hm = hext[HALO:HALO + T]
    xs_ref[...] = _mm(hext, w_ref[:, _XBC0:REST0])
    u_z = _mm(hm, w_ref[:, 0:_XBC0])
    u_r = _mm(hm, w_ref[:, REST0:N_PROJ])

    acc = jnp.broadcast_to(cb_ref[...], (T, SSD_CONV_CH))
    for j in range(SSD_CONV):
        acc = acc + cw_ref[j:j + 1, :] * xs_ref[pl.ds(HALO - SSD_CONV // 2 + j, T), :]
    xbc = acc * _sigmoid(acc)

    gq = u_r[:, 0:128] * (GLA_DK ** -0.5)
    gk = u_r[:, 128:256]
    gv = u_r[:, 256:512]
    gg = u_r[:, 512:768]
    rq = u_r[:, 768:1024]
    rk = u_r[:, 1024:1280]
    rv = u_r[:, 1280:1536]
    rg = u_r[:, 1536:1792]
    small = u_r[:, 1792:1920]

    lane128 = _iota((1, 128), 1)
    dt = _softplus(small + dtb_ref[...])
    la_s = dt * jnp.where(lane128 < 2 * SSD_HEADS, -jnp.exp(alog_ref[...]), 0.0)
    pre = _mm(small.astype(BF16), wsm_ref[...]) + ba_ref[...]
    la_g = -_softplus(-pre) * (1.0 / GLA_TEMP)

    cosv = cos_ref[...]
    sinv = sin_ref[...]
    first_half = (_iota((T, 256), 1) & 63) < 32

    def rotary(v):
        swapped = jnp.where(first_half, pltpu.roll(v, 224, 1), pltpu.roll(v, 32, 1))
        return v * cosv + swapped * sinv

    rq = rotary(rq)
    rk = rotary(rk) * (RET_DK ** -0.5)

    acts_ref[:, 0:1024] = xbc.astype(BF16)
    acts_ref[:, 1024:1536] = u_z.astype(BF16)
    acts_ref[:, 1536:1664] = gq.astype(BF16)
    acts_ref[:, 1664:1792] = gk.astype(BF16)
    acts_ref[:, 1792:2048] = gv.astype(BF16)
    acts_ref[:, 2048:2304] = gg.astype(BF16)
    acts_ref[:, 2304:2560] = rq.astype(BF16)
    acts_ref[:, 2560:2816] = rk.astype(BF16)
    acts_ref[:, 2816:3072] = rv.astype(BF16)
    acts_ref[:, 3072:3328] = rg.astype(BF16)
    dec_ref[:, 0:128] = dt
    dec_ref[:, 128:256] = la_s
    dec_ref[:, 256:512] = la_g

    sbs_ref[...] = st_s[...]
    sbg_ref[...] = st_g[...]
    sbr_ref[...] = st_r[...]

    tri = jnp.where(_iota((T, T), 0) >= _iota((T, T), 1), 1.0, 0.0).astype(BF16)
    lab = jnp.concatenate([la_s, la_g[:, 128:256]], axis=1)
    cum = _cumsum_rows(tri, lab)
    cbx = cum - lab
    tot = cum[T - 1:T, :]

    exp_b = _head_expander(SSD_HEADS, 2)
    wx = _expand(jnp.exp(cbx[:, 0:128]) * dt, exp_b)
    xw = (xbc[:, 0:SSD_WIDTH] * wx).astype(BF16)
    bm = xbc[:, 512:768].astype(BF16)
    dec_row = _expand(jnp.broadcast_to(jnp.exp(tot[:, 0:128]), (8, 128)), exp_b)[0:1]
    for g in range(2):
        rows = slice(128 * g, 128 * g + 128)
        cols = slice(256 * g, 256 * g + 256)
        st_s[rows, :] = st_s[rows, :] * dec_row[:, cols] + _mm_tn(bm[:, rows], xw[:, cols])

    mask_g = (_iota((256, 128), 0) >> 6) == (_iota((256, 128), 1) >> 5)
    kw = (gk * jnp.exp(cbx[:, 128:256])).astype(BF16)
    st_g[...] = st_g[...] * jnp.exp(tot[:, 128:256]) + jnp.where(mask_g, _mm_tn(gv.astype(BF16), kw), 0.0)

    mask_r = (_iota((256, 256), 0) >> 6) == (_iota((256, 256), 1) >> 6)
    lg = lg_ref[...]
    srow = _iota((T, 256), 0).astype(F32)
    kwr = (rk * jnp.exp(srow * lg)).astype(BF16)
    st_r[...] = st_r[...] * jnp.exp(float(T) * lg) + jnp.where(mask_r, _mm_tn(rv.astype(BF16), kwr), 0.0)


def _prep_call(h, w_p, cw, cb, dtb_row, alog_row, wsm, ba_row, cos_t, sin_t, lg_row):
    B, S, D = h.shape
    T = TILE
    nT = S // T
    hb = T // HALO
    const = lambda shape: pl.BlockSpec(shape, lambda b, i: (0,) * len(shape))
    in_specs = [
        pl.BlockSpec((None, HALO, D), lambda b, i: (b, jnp.maximum((nT - 1 - i) * hb - 1, 0), 0)),
        pl.BlockSpec((None, T, D), lambda b, i: (b, nT - 1 - i, 0)),
        pl.BlockSpec((None, HALO, D), lambda b, i: (b, jnp.minimum((nT - i) * hb, S // HALO - 1), 0)),
        const((D, N_PROJ)), const((SSD_CONV, SSD_CONV_CH)), const((1, SSD_CONV_CH)),
        const((1, 128)), const((1, 128)), const((128, 256)), const((1, 256)),
        pl.BlockSpec((T, 256), lambda b, i: (nT - 1 - i, 0)),
        pl.BlockSpec((T, 256), lambda b, i: (nT - 1 - i, 0)),
        const((1, 256)),
    ]
    out_specs = [
        pl.BlockSpec((None, T, N_ACT), lambda b, i: (b, nT - 1 - i, 0)),
        pl.BlockSpec((None, T, N_DEC), lambda b, i: (b, nT - 1 - i, 0)),
        pl.BlockSpec((None, None, 256, 256), lambda b, i: (b, nT - 1 - i, 0, 0)),
        pl.BlockSpec((None, None, 256, 128), lambda b, i: (b, nT - 1 - i, 0, 0)),
        pl.BlockSpec((None, None, 256, 256), lambda b, i: (b, nT - 1 - i, 0, 0)),
    ]
    out_shape = [
        jax.ShapeDtypeStruct((B, S, N_ACT), BF16),
        jax.ShapeDtypeStruct((B, S, N_DEC), F32),
        jax.ShapeDtypeStruct((B, nT, 256, 256), F32),
        jax.ShapeDtypeStruct((B, nT, 256, 128), F32),
        jax.ShapeDtypeStruct((B, nT, 256, 256), F32),
    ]
    scratch = [
        pltpu.VMEM((T + 2 * HALO, SSD_CONV_CH), F32),
        pltpu.VMEM((256, 256), F32), pltpu.VMEM((256, 128), F32), pltpu.VMEM((256, 256), F32),
    ]
    return pl.pallas_call(
        _prep_body,
        grid=(B, nT),
        in_specs=in_specs,
        out_specs=out_specs,
        out_shape=out_shape,
        scratch_shapes=scratch,
        compiler_params=pltpu.CompilerParams(
            dimension_semantics=("arbitrary", "arbitrary"), vmem_limit_bytes=VMEM_LIMIT),
        name="prep",
    )(h, h, h, w_p, cw, cb, dtb_row, alog_row, wsm, ba_row, cos_t, sin_t, lg_row)


def _group_mean(x, ones_blocks):
    return _mm(_split(x, 2), ones_blocks) * (1.0 / 64.0)


def _mix_body(acts_ref, dec_ref, sbs_ref, sbg_ref, sbr_ref, h_ref, p_ref,
              wout_ref, wpg_ref, wpe_ref, bpg_ref, lnw_ref, lnb_ref,
              snw_ref, gnw_ref, rnw_ref, rnb_ref, dsk_ref, lg_ref, lgs_ref,
              out_ref, sf_s, sf_g, sf_r, dm_ref):
    T = TILE
    b = pl.program_id(0)
    i = pl.program_id(1)

    @pl.when(i == 0)
    def _():
        sf_s[...] = jnp.zeros_like(sf_s)
        sf_g[...] = jnp.zeros_like(sf_g)
        sf_r[...] = jnp.zeros_like(sf_r)

    row = _iota((T, T), 0)
    col = _iota((T, T), 1)
    lower = row >= col

    @pl.when((b == 0) & (i == 0))
    def _():
        dist = jnp.abs(row - col).astype(F32)
        for hh in range(4):
            dm_ref[hh] = jnp.exp(dist * lgs_ref[hh])

    tri = jnp.where(lower, 1.0, 0.0).astype(BF16)
    mask_v = (_iota((256, 256), 0) >> 6) == (_iota((256, 256), 1) >> 6)
    mask_k = (_iota((256, 128), 0) >> 6) == (_iota((256, 128), 1) >> 5)
    lane_head = _iota((T, 256), 1) >> 6

    dt = dec_ref[:, 0:128]
    la = dec_ref[:, 128:256]
    cum = _cumsum_rows(tri, la)
    cbx = cum - la
    tot = cum[T - 1:T, :]
    cum_t = cum.T
    cbx_t = cbx.T
    dt_t = dt.T
    exp_f = _head_expander(0, 2)
    exp_b = _head_expander(SSD_HEADS, 2)
    e_f = _expand(jnp.exp(cum), exp_f)
    e_b = _expand(jnp.exp(tot - cbx), exp_b)
    w_f = _expand(jnp.exp(tot - cum) * dt, exp_f)
    dec_f = _expand(jnp.broadcast_to(jnp.exp(tot), (8, 128)), exp_f)[0:1]
    x_bf = acts_ref[:, 0:512]
    b_bf = acts_ref[:, 512:768]
    c_bf = acts_ref[:, 768:1024]
    x_f = x_bf.astype(F32)
    xw = (x_f * w_f).astype(BF16)
    y_groups = []
    for g in range(2):
        rows = slice(128 * g, 128 * g + 128)
        cols = slice(256 * g, 256 * g + 256)
        c_g = c_bf[:, rows]
        b_g = b_bf[:, rows]
        x_g = x_bf[:, cols]
        raw = _mm_nt(c_g, b_g)
        y_g = jnp.zeros((T, 256), F32)
        for hh in range(4):
            hd = 4 * g + hh
            hb = SSD_HEADS + hd
            arg = jnp.where(lower, cum[:, hd:hd + 1] - cum_t[hd:hd + 1, :],
                            cbx_t[hb:hb + 1, :] - cbx[:, hb:hb + 1])
            m = jnp.exp(arg) * jnp.where(lower, dt_t[hd:hd + 1, :], dt_t[hb:hb + 1, :])
            s = (raw * m).astype(BF16)
            y_g = y_g + _mm(s, jnp.where(lane_head == hh, x_g, jnp.zeros_like(x_g)))
        st = jnp.concatenate([sf_s[rows, :], sbs_ref[rows, :]], axis=1).astype(BF16)
        cs = _mm(c_g, st)
        y_g = y_g + e_f[:, cols] * cs[:, 0:256] + e_b[:, cols] * cs[:, 256:512]
        sf_s[rows, :] = sf_s[rows, :] * dec_f[:, cols] + _mm_tn(b_g, xw[:, cols])
        y_groups.append(y_g)
    y_ssd = jnp.concatenate(y_groups, axis=1) + dsk_ref[...] * x_f
    z = acts_ref[:, 1024:1536].astype(F32)
    yz = y_ssd * (z * _sigmoid(z))
    y1 = yz * lax.rsqrt(jnp.mean(yz * yz, axis=-1, keepdims=True) + RMS_EPS) * snw_ref[...]

    L = GLA_CHUNK
    nc = T // L
    laf = dec_ref[:, 256:384]
    lab = dec_ref[:, 384:512]
    tri_c = jnp.where(lower & ((row >> 6) == (col >> 6)), 1.0, 0.0).astype(BF16)
    cg = _cumsum_rows(tri_c, jnp.concatenate([laf, lab], axis=1))
    cf = cg[:, 0:128]
    cb = cg[:, 128:256]
    cbx_g = cb - lab
    last_rows = [cg[L * c + L - 1:L * c + L, :] for c in range(nc)]
    last_full = jnp.concatenate([jnp.broadcast_to(r, (L, 256)) for r in last_rows], axis=0)
    q = acts_ref[:, 1536:1664].astype(F32)
    k = acts_ref[:, 1664:1792].astype(F32)
    v_bf = acts_ref[:, 1792:2048]
    ecf = jnp.exp(cf)
    ecbx = jnp.exp(cbx_g)
    qf = (q * ecf).astype(BF16)
    kf = (k * jnp.exp(-cf)).astype(BF16)
    qb = (q * jnp.exp(-cbx_g)).astype(BF16)
    kb = (k * ecbx).astype(BF16)
    ksf = (k * jnp.exp(last_full[:, 0:128] - cf)).astype(BF16)
    qbi = (q * jnp.exp(last_full[:, 128:256] - cbx_g)).astype(BF16)
    lower_c = _iota((L, 256), 0) >= (_iota((L, 256), 1) & (L - 1))
    mask_kk = (_iota((256, 256), 0) >> 6) == ((_iota((256, 256), 1) & 127) >> 5)
    y_chunks, contribs = [], []
    for c in range(nc):
        sl = slice(L * c, L * c + L)
        kf_t = jnp.where(mask_k, _tile_rows(kf[sl], 4), jnp.zeros((256, 128), BF16))
        kb_t = jnp.where(mask_k, _tile_rows(kb[sl], 4), jnp.zeros((256, 128), BF16))
        sc = jnp.where(lower_c, _mm_nt(qf[sl], kf_t), _mm_nt(qb[sl], kb_t)).astype(BF16)
        vd = jnp.where(mask_v, _tile_rows(v_bf[sl], 4), jnp.zeros((256, 256), BF16))
        y_chunks.append(_mm(sc, vd))
        a = _mm_tn(v_bf[sl], jnp.concatenate([ksf[sl], kb[sl]], axis=1))
        contribs.append(jnp.where(mask_kk, a, 0.0))
    s_f, s_b = [None] * nc, [None] * nc
    s = sf_g[...]
    for c in range(nc):
        s_f[c] = s
        s = s * jnp.exp(last_rows[c][:, 0:128]) + contribs[c][:, 0:128]
    sf_g[...] = s
    s = sbg_ref[...]
    for c in reversed(range(nc)):
        s_b[c] = s
        s = s * jnp.exp(last_rows[c][:, 128:256]) + contribs[c][:, 128:256]
    for c in range(nc):
        sl = slice(L * c, L * c + L)
        qq = jnp.concatenate([qf[sl], qbi[sl]], axis=1)
        ss = jnp.concatenate([s_f[c], s_b[c]], axis=1).astype(BF16)
        y_chunks[c] = y_chunks[c] + _mm_nt(qq, ss)
    o_g = jnp.concatenate(y_chunks, axis=0)
    ones_blocks = jnp.where(_tile_rows(mask_v, 2), 1.0, 0.0).astype(BF16)
    gg = acts_ref[:, 2048:2304].astype(F32)
    ms = _group_mean(o_g * o_g, ones_blocks)
    y2 = o_g * lax.rsqrt(ms + RMS_EPS) * gnw_ref[...] * (gg * _sigmoid(gg))

    q_bf = acts_ref[:, 2304:2560]
    k_bf = acts_ref[:, 2560:2816]
    v_r = acts_ref[:, 2816:3072]
    zero_bf = jnp.zeros((T, 256), BF16)
    o_r = jnp.zeros((T, 256), F32)
    for hh in range(4):
        sel = lane_head == hh
        raw = _mm_nt(jnp.where(sel, q_bf, zero_bf), k_bf)
        o_r = o_r + _mm((raw * dm_ref[hh]).astype(BF16), jnp.where(sel, v_r, zero_bf))
    lg = lg_ref[...]
    tpos = _iota((T, 256), 0).astype(F32)
    q_f32 = q_bf.astype(F32)
    qq = jnp.concatenate([q_f32 * jnp.exp((tpos + 1.0) * lg),
                          q_f32 * jnp.exp((float(T) - tpos) * lg)], axis=1).astype(BF16)
    ss = jnp.concatenate([sf_r[...], sbr_ref[...]], axis=1).astype(BF16)
    o_r = o_r + _mm_nt(qq, ss)
    kwf = (k_bf.astype(F32) * jnp.exp((float(T - 1) - tpos) * lg)).astype(BF16)
    sf_r[...] = sf_r[...] * jnp.exp(float(T) * lg) + jnp.where(mask_v, _mm_tn(v_r, kwf), 0.0)
    mu = _group_mean(o_r, ones_blocks)
    d = o_r - mu
    var = _group_mean(d * d, ones_blocks)
    rg = acts_ref[:, 3072:3328].astype(F32)
    y3 = (d * lax.rsqrt(var + LN_EPS) * rnw_ref[...] + rnb_ref[...]) * (rg * _sigmoid(rg))

    mix = (_mm(y1.astype(BF16), wout_ref[0:512, :]) + _mm(y2.astype(BF16), wout_ref[512:768, :])
           + _mm(y3.astype(BF16), wout_ref[768:1024, :]))
    r = DN_ALPHA * h_ref[...] + mix
    mu_r = jnp.mean(r, axis=-1, keepdims=True)
    dr = r - mu_r
    var_r = jnp.mean(dr * dr, axis=-1, keepdims=True)
    hn = dr * lax.rsqrt(var_r + LN_EPS) * lnw_ref[...] + lnb_ref[...]
    gate = _sigmoid(_mm(hn.astype(BF16), wpg_ref[...]) + bpg_ref[...])
    out_ref[...] = hn + gate * _mm(p_ref[...].astype(BF16), wpe_ref[...])


def _mix_call(acts, dec, sbs, sbg, sbr, h, p_i, wout, wpg, wpe, bpg, lnw, lnb,
              snw, gnw, rnw, rnb, dsk, lg_row, lg_vec):
    B, S, D = h.shape
    T = TILE
    nT = S // T
    const = lambda shape: pl.BlockSpec(shape, lambda b, i: (0,) * len(shape))
    tile = lambda width: pl.BlockSpec((None, T, width), lambda b, i: (b, i, 0))
    state = lambda width: pl.BlockSpec((None, None, 256, width), lambda b, i: (b, i, 0, 0))
    in_specs = [
        tile(N_ACT), tile(N_DEC), state(256), state(128), state(256), tile(D), tile(D_PLE),
        const((D, D)), const((D, D)), const((D_PLE, D)), const((1, D)), const((1, D)), const((1, D)),
        const((1, SSD_WIDTH)), const((1, 256)), const((1, 256)), const((1, 256)), const((1, SSD_WIDTH)),
        const((1, 256)),
        pl.BlockSpec(memory_space=pltpu.SMEM),
    ]
    scratch = [
        pltpu.VMEM((256, 256), F32), pltpu.VMEM((256, 128), F32), pltpu.VMEM((256, 256), F32),
        pltpu.VMEM((4, T, T), F32),
    ]
    return pl.pallas_call(
        _mix_body,
        grid=(B, nT),
        in_specs=in_specs,
        out_specs=tile(D),
        out_shape=jax.ShapeDtypeStruct((B, S, D), F32),
        scratch_shapes=scratch,
        compiler_params=pltpu.CompilerParams(
            dimension_semantics=("arbitrary", "arbitrary"), vmem_limit_bytes=VMEM_LIMIT),
        name="mix",
    )(acts, dec, sbs, sbg, sbr, h, p_i, wout, wpg, wpe, bpg, lnw, lnb, snw, gnw, rnw, rnb, dsk,
      lg_row, lg_vec)


def _permute_proj(w):
    pieces = [w[:, _Z0:_DT0], w[:, _GQ0:_GA0], w[:, _RQ0:_NIN], w[:, _DT0:_GQ0], w[:, _GA0:_RQ0],
              jnp.zeros((w.shape[0], N_PROJ - _NIN), w.dtype)]
    return jnp.concatenate(pieces, axis=1).astype(BF16)


def kernel(x, p, w_in, conv_w, conv_b, dt_bias, a_log, d_skip, ssd_norm_w, gla_w_a2, gla_b_a,
           gla_norm_w, ret_norm_w, ret_norm_b, w_out, ln_w, ln_b, w_pe, w_pg, b_pg):
    B, S, D = x.shape
    assert D == D_MODEL and S % TILE == 0 and TILE % GLA_CHUNK == 0
    half = RET_DK // 2
    inv = ROPE_BASE ** (-jnp.arange(half, dtype=F32) / half)
    cos_t, sin_t = _rope_table(jnp.tile(inv, 256 // half)[None, :], S)
    log_gamma = jnp.log(1.0 - 2.0 ** (-5.0 - jnp.arange(4, dtype=F32)))
    lg_row = jnp.repeat(log_gamma, 64)[None, :]
    pad112 = jnp.zeros((112,), F32)
    h = x
    for i in range(DEPTH):
        w_p = _permute_proj(w_in[i])
        dtb_row = jnp.concatenate([dt_bias[i].reshape(16), pad112])[None, :]
        alog_row = jnp.concatenate([a_log[i].reshape(16), pad112])[None, :]
        wsm = jnp.zeros((128, 256), F32)
        wsm = wsm.at[16:32, 0:128].set(gla_w_a2[i, 0]).at[32:48, 128:256].set(gla_w_a2[i, 1]).astype(BF16)
        acts, dec, sbs, sbg, sbr = _prep_call(
            h, w_p, conv_w[i], conv_b[i][None, :], dtb_row, alog_row, wsm,
            gla_b_a[i].reshape(1, 256), cos_t, sin_t, lg_row)
        h = _mix_call(
            acts, dec, sbs, sbg, sbr, h, p[i],
            w_out[i].astype(BF16), w_pg[i].astype(BF16), w_pe[i].astype(BF16), b_pg[i][None, :],
            ln_w[i][None, :], ln_b[i][None, :], ssd_norm_w[i][None, :],
            jnp.tile(gla_norm_w[i], 4)[None, :], ret_norm_w[i][None, :], ret_norm_b[i][None, :],
            jnp.repeat(d_skip[i], 64)[None, :], lg_row, log_gamma)
    return h
```

```python
import functools

import jax
import jax.numpy as jnp
from jax import lax
from jax.experimental import pallas as pl
from jax.experimental.pallas import tpu as pltpu

F32 = jnp.float32
BF16 = jnp.bfloat16

D_MODEL = 1024
D_PLE = 256
DEPTH = 2
SSD_HEADS = 8
SSD_WIDTH = 512
SSD_STATE = 128
SSD_CONV = 5
SSD_CONV_CH = 1024
GLA_DK = 32
GLA_TEMP = 16.0
RET_DK = 64
ROPE_BASE = 10000.0
DN_ALPHA = float((2 * DEPTH) ** 0.25)
LN_EPS = 1e-5
RMS_EPS = 1e-6
LOG2E = 1.4426950408889634

_Z0, _XBC0, _DT0, _GQ0, _GK0, _GV0, _GG0, _GA0, _RQ0, _RK0, _RV0, _RG0, _NIN = (
    0, 512, 1536, 1552, 1680, 1808, 2064, 2320, 2352, 2608, 2864, 3120, 3376)
N_PROJ = 3456
REST0 = 1536
N_ACT = 3328
N_DEC = 512

TILE = 256
HALO = 16
GLA_CHUNK = 64
VMEM_LIMIT = 56 * 1024 * 1024
MIX_FLAGS = None


def _mm(a, b):
    return jnp.dot(a, b, preferred_element_type=F32)


def _mm_nt(a, b):
    return lax.dot_general(a, b, (((1,), (1,)), ((), ())), preferred_element_type=F32)


def _mm_tn(a, b):
    return lax.dot_general(a, b, (((0,), (0,)), ((), ())), preferred_element_type=F32)


def _iota(shape, dim):
    return lax.broadcasted_iota(jnp.int32, shape, dim)


def _split(x, parts):
    out, r = [], x
    for k in range(parts):
        piece = r.astype(BF16)
        out.append(piece)
        if k + 1 < parts:
            r = r - piece.astype(F32)
    return jnp.concatenate(out, axis=1)


def _fold(y, parts):
    w = y.shape[1] // parts
    acc = y[:, :w]
    for k in range(1, parts):
        acc = acc + y[:, k * w:(k + 1) * w]
    return acc


def _cumsum_rows(tri, x):
    return _fold(_mm(tri, _split(x, 3)), 3)


def _head_expander(first_lane, parts):
    r = _iota((128 * parts, SSD_WIDTH), 0) & 127
    c = _iota((128 * parts, SSD_WIDTH), 1) >> 6
    return jnp.where(r == c + first_lane, 1.0, 0.0).astype(BF16)


def _expand(blk, expander):
    return _mm(_split(blk, 2), expander)


def _sigmoid(x):
    return 1.0 / (1.0 + jnp.exp(-x))


def _softplus(x):
    return jnp.maximum(x, 0.0) + jnp.log1p(jnp.exp(-jnp.abs(x)))


def _tile_rows(x, n):
    return jnp.concatenate([x] * n, axis=0)


def _rope_body(inv_ref, cos_ref, sin_ref):
    rows = cos_ref.shape[0]
    pos = (_iota((rows, 256), 0) + pl.program_id(0) * rows).astype(F32)
    ang = pos * inv_ref[...]
    first_half = (_iota((rows, 256), 1) & 63) < 32
    s = jnp.sin(ang)
    cos_ref[...] = jnp.cos(ang)
    sin_ref[...] = jnp.where(first_half, -s, s)


def _rope_table(inv_row, seq):
    rows = 512
    return pl.pallas_call(
        _rope_body,
        grid=(seq // rows,),
        in_specs=[pl.BlockSpec((1, 256), lambda i: (0, 0))],
        out_specs=[pl.BlockSpec((rows, 256), lambda i: (i, 0))] * 2,
        out_shape=[jax.ShapeDtypeStruct((seq, 256), F32)] * 2,
        name="rope_table",
    )(inv_row)


def _prep_body(hp_ref, h_ref, hn_ref, w_ref, cw_ref, cb_ref, dtb_ref, alog_ref, wsm_ref, ba_ref,
               cos_ref, sin_ref, lg_ref,
               acts_ref, dec_ref, sbs_ref, sbg_ref, sbr_ref,
               xs_ref, st_s, st_g, st_r):
    T = TILE
    i = pl.program_id(1)
    n = pl.num_programs(1)
    t = n - 1 - i

    @pl.when(i == 0)
    def _():
        st_s[...] = jnp.zeros_like(st_s)
        st_g[...] = jnp.zeros_like(st_g)
        st_r[...] = jnp.zeros_like(st_r)

    hp = jnp.where(t > 0, hp_ref[...], 0.0)
    hn = jnp.where(t < n - 1, hn_ref[...], 0.0)
    hext = jnp.concatenate([hp, h_ref[...], hn], axis=0).astype(BF16)
    hm = hext[HALO:HALO + T]
    xs = _mm(hext, w_ref[:, _XBC0:REST0])
    u_z = _mm(hm, w_ref[:, 0:_XBC0])
    u_r = _mm(hm, w_ref[:, REST0:N_PROJ])

    acc = jnp.broadcast_to(cb_ref[...], (T, SSD_CONV_CH))
    for j in range(SSD_CONV):
        first = HALO - SSD_CONV // 2 + j
        acc = acc + cw_ref[j:j + 1, :] * xs[first:first + T, :]
    xbc = acc * _sigmoid(acc)

    gq = u_r[:, 0:128] * (GLA_DK ** -0.5)
    gk = u_r[:, 128:256]
    gv = u_r[:, 256:512]
    gg = u_r[:, 512:768]
    rq = u_r[:, 768:1024]
    rk = u_r[:, 1024:1280]
    rv = u_r[:, 1280:1536]
    rg = u_r[:, 1536:1792]
    small = u_r[:, 1792:1920]

    lane128 = _iota((1, 128), 1)
    dt = _softplus(small + dtb_ref[...])
    la_s = dt * jnp.where(lane128 < 2 * SSD_HEADS, -jnp.exp(alog_ref[...]), 0.0)
    pre = _mm(small.astype(BF16), wsm_ref[...]) + ba_ref[...]
    la_g = -_softplus(-pre) * (1.0 / GLA_TEMP)

    cosv = cos_ref[...]
    sinv = sin_ref[...]
    first_half = (_iota((T, 256), 1) & 63) < 32

    def rotary(v):
        swapped = jnp.where(first_half, pltpu.roll(v, 224, 1), pltpu.roll(v, 32, 1))
        return v * cosv + swapped * sinv

    rq = rotary(rq)
    rk = rotary(rk) * (RET_DK ** -0.5)

    acts_ref[:, 0:1024] = xbc.astype(BF16)
    acts_ref[:, 1024:1536] = u_z.astype(BF16)
    acts_ref[:, 1536:1664] = gq.astype(BF16)
    acts_ref[:, 1664:1792] = gk.astype(BF16)
    acts_ref[:, 1792:2048] = gv.astype(BF16)
    acts_ref[:, 2048:2304] = gg.astype(BF16)
    acts_ref[:, 2304:2560] = rq.astype(BF16)
    acts_ref[:, 2560:2816] = rk.astype(BF16)
    acts_ref[:, 2816:3072] = rv.astype(BF16)
    acts_ref[:, 3072:3328] = rg.astype(BF16)
    dec_ref[:, 0:128] = dt
    dec_ref[:, 128:256] = la_s
    dec_ref[:, 256:512] = la_g

    sbs_ref[...] = st_s[...]
    sbg_ref[...] = st_g[...]
    sbr_ref[...] = st_r[...]

    tri = jnp.where(_iota((T, T), 0) >= _iota((T, T), 1), 1.0, 0.0).astype(BF16)
    lab = jnp.concatenate([la_s, la_g[:, 128:256]], axis=1)
    cum = _cumsum_rows(tri, lab)
    cbx = cum - lab
    tot = cum[T - 1:T, :]

    exp_b = _head_expander(SSD_HEADS, 2)
    wx = _expand(jnp.exp(cbx[:, 0:128]) * dt, exp_b)
    xw = (xbc[:, 0:SSD_WIDTH] * wx).astype(BF16)
    bm = xbc[:, 512:768].astype(BF16)
    dec_row = _expand(jnp.broadcast_to(jnp.exp(tot[:, 0:128]), (8, 128)), exp_b)[0:1]
    for g in range(2):
        rows = slice(128 * g, 128 * g + 128)
        cols = slice(256 * g, 256 * g + 256)
        st_s[rows, :] = st_s[rows, :] * dec_row[:, cols] + _mm_tn(bm[:, rows], xw[:, cols])

    mask_g = (_iota((256, 128), 0) >> 6) == (_iota((256, 128), 1) >> 5)
    kw = (gk * jnp.exp(cbx[:, 128:256])).astype(BF16)
    st_g[...] = st_g[...] * jnp.exp(tot[:, 128:256]) + jnp.where(mask_g, _mm_tn(gv.astype(BF16), kw), 0.0)

    mask_r = (_iota((256, 256), 0) >> 6) == (_iota((256, 256), 1) >> 6)
    lg = lg_ref[...]
    srow = _iota((T, 256), 0).astype(F32)
    kwr = (rk * jnp.exp(srow * lg)).astype(BF16)
    st_r[...] = st_r[...] * jnp.exp(float(T) * lg) + jnp.where(mask_r, _mm_tn(rv.astype(BF16), kwr), 0.0)


def _prep_call(h, w_p, cw, cb, dtb_row, alog_row, wsm, ba_row, cos_t, sin_t, lg_row):
    B, S, D = h.shape
    T = TILE
    nT = S // T
    hb = T // HALO
    const = lambda shape: pl.BlockSpec(shape, lambda b, i: (0,) * len(shape))
    in_specs = [
        pl.BlockSpec((None, HALO, D), lambda b, i: (b, jnp.maximum((nT - 1 - i) * hb - 1, 0), 0)),
        pl.BlockSpec((None, T, D), lambda b, i: (b, nT - 1 - i, 0)),
        pl.BlockSpec((None, HALO, D), lambda b, i: (b, jnp.minimum((nT - i) * hb, S // HALO - 1), 0)),
        const((D, N_PROJ)), const((SSD_CONV, SSD_CONV_CH)), const((1, SSD_CONV_CH)),
        const((1, 128)), const((1, 128)), const((128, 256)), const((1, 256)),
        pl.BlockSpec((T, 256), lambda b, i: (nT - 1 - i, 0)),
        pl.BlockSpec((T, 256), lambda b, i: (nT - 1 - i, 0)),
        const((1, 256)),
    ]
    out_specs = [
        pl.BlockSpec((None, T, N_ACT), lambda b, i: (b, nT - 1 - i, 0)),
        pl.BlockSpec((None, T, N_DEC), lambda b, i: (b, nT - 1 - i, 0)),
        pl.BlockSpec((None, None, 256, 256), lambda b, i: (b, nT - 1 - i, 0, 0)),
        pl.BlockSpec((None, None, 256, 128), lambda b, i: (b, nT - 1 - i, 0, 0)),
        pl.BlockSpec((None, None, 256, 256), lambda b, i: (b, nT - 1 - i, 0, 0)),
    ]
    out_shape = [
        jax.ShapeDtypeStruct((B, S, N_ACT), BF16),
        jax.ShapeDtypeStruct((B, S, N_DEC), F32),
        jax.ShapeDtypeStruct((B, nT, 256, 256), F32),
        jax.ShapeDtypeStruct((B, nT, 256, 128), F32),
        jax.ShapeDtypeStruct((B, nT, 256, 256), F32),
    ]
    scratch = [
        pltpu.VMEM((T + 2 * HALO, SSD_CONV_CH), F32),
        pltpu.VMEM((256, 256), F32), pltpu.VMEM((256, 128), F32), pltpu.VMEM((256, 256), F32),
    ]
    return pl.pallas_call(
        _prep_body,
        grid=(B, nT),
        in_specs=in_specs,
        out_specs=out_specs,
        out_shape=out_shape,
        scratch_shapes=scratch,
        compiler_params=pltpu.CompilerParams(
            dimension_semantics=("arbitrary", "arbitrary"), vmem_limit_bytes=VMEM_LIMIT),
        name="prep",
    )(h, h, h, w_p, cw, cb, dtb_row, alog_row, wsm, ba_row, cos_t, sin_t, lg_row)


def _group_mean(x, ones_blocks):
    return _mm(x.astype(BF16), ones_blocks) * (1.0 / 64.0)


def _mix_body(acts_ref, dec_ref, sbs_ref, sbg_ref, sbr_ref, h_ref, p_ref,
              wout_ref, wpg_ref, wpe_ref, bpg_ref, lnw_ref, lnb_ref,
              snw_ref, gnw_ref, rnw_ref, rnb_ref, dsk_ref, lg_ref, lgs_ref,
              out_ref, sf_s, sf_g, sf_r, dm_ref, ycat_ref, *, tiles_per_seq):
    T = TILE
    step = pl.program_id(0)
    scan_tile = jnp.minimum(step, pl.num_programs(0) - 2)

    @pl.when(lax.rem(scan_tile, tiles_per_seq) == 0)
    def _():
        sf_s[...] = jnp.zeros_like(sf_s)
        sf_g[...] = jnp.zeros_like(sf_g)
        sf_r[...] = jnp.zeros_like(sf_r)

    row = _iota((T, T), 0)
    col = _iota((T, T), 1)
    lower = row >= col

    @pl.when(step == 0)
    def _():
        ycat_ref[...] = jnp.zeros_like(ycat_ref)
        dist = jnp.abs(row - col).astype(F32)
        for hh in range(4):
            dm_ref[hh] = jnp.exp(dist * lgs_ref[hh])


    tri = jnp.where(lower, 1.0, 0.0).astype(BF16)
    mask_v = (_iota((256, 256), 0) >> 6) == (_iota((256, 256), 1) >> 6)
    mask_k = (_iota((256, 128), 0) >> 6) == (_iota((256, 128), 1) >> 5)
    lane_head = _iota((T, 256), 1) >> 6
    L = GLA_CHUNK
    nc = T // L
    group_rows = [slice(128 * g, 128 * g + 128) for g in range(2)]
    group_cols = [slice(256 * g, 256 * g + 256) for g in range(2)]

    dt = dec_ref[:, 0:128]
    la = dec_ref[:, 128:256]
    cum = _cumsum_rows(tri, la)
    laf = dec_ref[:, 256:384]
    lab = dec_ref[:, 384:512]
    tri_c = jnp.where(lower & ((row >> 6) == (col >> 6)), 1.0, 0.0).astype(BF16)
    cg = _cumsum_rows(tri_c, jnp.concatenate([laf, lab], axis=1))
    x_bf = acts_ref[:, 0:512]
    b_bf = acts_ref[:, 512:768]
    c_bf = acts_ref[:, 768:1024]
    raw_s = [_mm_nt(c_bf[:, group_rows[g]], b_bf[:, group_rows[g]]) for g in range(2)]
    q_bf = acts_ref[:, 2304:2560]
    k_bf = acts_ref[:, 2560:2816]
    v_r = acts_ref[:, 2816:3072]
    zero_bf = jnp.zeros((T, 256), BF16)
    raw_r = [_mm_nt(jnp.where(lane_head == hh, q_bf, zero_bf), k_bf) for hh in range(4)]

    mix = _mm(ycat_ref[...], wout_ref[...])
    pe = _mm(p_ref[...].astype(BF16), wpe_ref[...])

    cbx = cum - la
    tot = cum[T - 1:T, :]
    ldt = jnp.log(dt)
    col_f = cum * LOG2E
    col_b = cbx * LOG2E
    row_f = ((cum - ldt) * LOG2E).T
    row_b = ((cbx + ldt) * LOG2E).T
    H = T // 2
    lower_h = lower[0:H, 0:H]
    exp_f = _head_expander(0, 2)
    exp_b = _head_expander(SSD_HEADS, 2)
    e_f = _expand(jnp.exp(cum), exp_f)
    e_b = _expand(jnp.exp(tot - cbx), exp_b)
    w_f = _expand(jnp.exp(tot - cum) * dt, exp_f)
    dec_f = _expand(jnp.broadcast_to(jnp.exp(tot), (8, 128)), exp_f)[0:1]
    x_f = x_bf.astype(F32)
    xw = (x_f * w_f).astype(BF16)
    y_groups = []
    for g in range(2):
        x_g = x_bf[:, group_cols[g]]
        y_g = jnp.zeros((T, 256), F32)
        for hh in range(4):
            hd = 4 * g + hh
            hb = SSD_HEADS + hd
            cf_h, cb_h = col_f[:, hd:hd + 1], col_b[:, hb:hb + 1]
            rf_h, rb_h = row_f[hd:hd + 1, :], row_b[hb:hb + 1, :]
            top = jnp.concatenate(
                [jnp.where(lower_h, cf_h[0:H] - rf_h[:, 0:H], rb_h[:, 0:H] - cb_h[0:H]),
                 rb_h[:, H:T] - cb_h[0:H]], axis=1)
            bot = jnp.concatenate(
                [cf_h[H:T] - rf_h[:, 0:H],
                 jnp.where(lower_h, cf_h[H:T] - rf_h[:, H:T], rb_h[:, H:T] - cb_h[H:T])], axis=1)
            m = jnp.exp2(jnp.concatenate([top, bot], axis=0))
            s = (raw_s[g] * m).astype(BF16)
            y_g = y_g + _mm(s, jnp.where(lane_head == hh, x_g, jnp.zeros_like(x_g)))
        y_groups.append(y_g)

    o_r = jnp.zeros((T, 256), F32)
    for hh in range(4):
        o_r = o_r + _mm((raw_r[hh] * dm_ref[hh]).astype(BF16), jnp.where(lane_head == hh, v_r, zero_bf))

    r = DN_ALPHA * h_ref[...] + mix
    mu_r = jnp.mean(r, axis=-1, keepdims=True)
    dr = r - mu_r
    var_r = jnp.mean(dr * dr, axis=-1, keepdims=True)
    hn = dr * lax.rsqrt(var_r + LN_EPS) * lnw_ref[...] + lnb_ref[...]
    gate_pre = _mm(hn.astype(BF16), wpg_ref[...])

    for g in range(2):
        rows, cols = group_rows[g], group_cols[g]
        st = jnp.concatenate([sf_s[rows, :], sbs_ref[rows, :]], axis=1).astype(BF16)
        cs = _mm(c_bf[:, rows], st)
        y_groups[g] = y_groups[g] + e_f[:, cols] * cs[:, 0:256] + e_b[:, cols] * cs[:, 256:512]
        sf_s[rows, :] = sf_s[rows, :] * dec_f[:, cols] + _mm_tn(b_bf[:, rows], xw[:, cols])
    y_ssd = jnp.concatenate(y_groups, axis=1) + dsk_ref[...] * x_f
    z = acts_ref[:, 1024:1536].astype(F32)
    yz = y_ssd * (z * _sigmoid(z))
    y1 = yz * lax.rsqrt(jnp.mean(yz * yz, axis=-1, keepdims=True) + RMS_EPS) * snw_ref[...]

    lg = lg_ref[...]
    tpos = _iota((T, 256), 0).astype(F32)
    q_f32 = q_bf.astype(F32)
    qq_r = jnp.concatenate([q_f32 * jnp.exp((tpos + 1.0) * lg),
                            q_f32 * jnp.exp((float(T) - tpos) * lg)], axis=1).astype(BF16)
    ss_r = jnp.concatenate([sf_r[...], sbr_ref[...]], axis=1).astype(BF16)
    o_r = o_r + _mm_nt(qq_r, ss_r)
    kwf = (k_bf.astype(F32) * jnp.exp((float(T - 1) - tpos) * lg)).astype(BF16)
    sf_r[...] = sf_r[...] * jnp.exp(float(T) * lg) + jnp.where(mask_v, _mm_tn(v_r, kwf), 0.0)

    out_ref[...] = hn + _sigmoid(gate_pre + bpg_ref[...]) * pe

    cf = cg[:, 0:128]
    cb = cg[:, 128:256]
    cbx_g = cb - lab
    last_rows = [cg[L * c + L - 1:L * c + L, :] for c in range(nc)]
    last_full = jnp.concatenate([jnp.broadcast_to(r, (L, 256)) for r in last_rows], axis=0)
    q = acts_ref[:, 1536:1664].astype(F32)
    k = acts_ref[:, 1664:1792].astype(F32)
    v_bf = acts_ref[:, 1792:2048]
    ecf = jnp.exp(cf)
    ecbx = jnp.exp(cbx_g)
    qf = (q * ecf).astype(BF16)
    kf = (k * jnp.exp(-cf)).astype(BF16)
    qb = (q * jnp.exp(-cbx_g)).astype(BF16)
    kb = (k * ecbx).astype(BF16)
    ksf = (k * jnp.exp(last_full[:, 0:128] - cf)).astype(BF16)
    qbi = (q * jnp.exp(last_full[:, 128:256] - cbx_g)).astype(BF16)
    lower_c = _iota((L, 256), 0) >= (_iota((L, 256), 1) & (L - 1))
    mask_kk = (_iota((256, 256), 0) >> 6) == ((_iota((256, 256), 1) & 127) >> 5)
    y_chunks, contribs = [], []
    for c in range(nc):
        sl = slice(L * c, L * c + L)
        kf_t = jnp.where(mask_k, _tile_rows(kf[sl], 4), jnp.zeros((256, 128), BF16))
        kb_t = jnp.where(mask_k, _tile_rows(kb[sl], 4), jnp.zeros((256, 128), BF16))
        sc = jnp.where(lower_c, _mm_nt(qf[sl], kf_t), _mm_nt(qb[sl], kb_t)).astype(BF16)
        vd = jnp.where(mask_v, _tile_rows(v_bf[sl], 4), jnp.zeros((256, 256), BF16))
        y_chunks.append(_mm(sc, vd))
        a = _mm_tn(v_bf[sl], jnp.concatenate([ksf[sl], kb[sl]], axis=1))
        contribs.append(jnp.where(mask_kk, a, 0.0))
    s_f, s_b = [None] * nc, [None] * nc
    s = sf_g[...]
    for c in range(nc):
        s_f[c] = s
        s = s * jnp.exp(last_rows[c][:, 0:128]) + contribs[c][:, 0:128]
    sf_g[...] = s
    s = sbg_ref[...]
    for c in reversed(range(nc)):
        s_b[c] = s
        s = s * jnp.exp(last_rows[c][:, 128:256]) + contribs[c][:, 128:256]
    for c in range(nc):
        sl = slice(L * c, L * c + L)
        qq = jnp.concatenate([qf[sl], qbi[sl]], axis=1)
        ss = jnp.concatenate([s_f[c], s_b[c]], axis=1).astype(BF16)
        y_chunks[c] = y_chunks[c] + _mm_nt(qq, ss)
    o_g = jnp.concatenate(y_chunks, axis=0)
    ones_blocks = jnp.where(mask_v, 1.0, 0.0).astype(BF16)
    gg = acts_ref[:, 2048:2304].astype(F32)
    ms = _group_mean(o_g * o_g, ones_blocks)
    y2 = o_g * lax.rsqrt(ms + RMS_EPS) * gnw_ref[...] * (gg * _sigmoid(gg))

    mu = _group_mean(o_r, ones_blocks)
    d = o_r - mu
    var = _group_mean(d * d, ones_blocks)
    rg = acts_ref[:, 3072:3328].astype(F32)
    y3 = (d * lax.rsqrt(var + LN_EPS) * rnw_ref[...] + rnb_ref[...]) * (rg * _sigmoid(rg))

    ycat_ref[:, 0:512] = y1.astype(BF16)
    ycat_ref[:, 512:768] = y2.astype(BF16)
    ycat_ref[:, 768:1024] = y3.astype(BF16)


def _mix_call(acts, dec, sbs, sbg, sbr, h, p, layer, wout, wpg, wpe, bpg, lnw, lnb,
              snw, gnw, rnw, rnb, dsk, lg_row, lg_vec):
    B, S, D = h.shape
    T = TILE
    nT = S // T
    n_tiles = B * nT
    scan_bt = lambda s: (lax.div(jnp.minimum(s, n_tiles - 1), nT), lax.rem(jnp.minimum(s, n_tiles - 1), nT))
    fin_bt = lambda s: (lax.div(jnp.maximum(s - 1, 0), nT), lax.rem(jnp.maximum(s - 1, 0), nT))
    const = lambda shape: pl.BlockSpec(shape, lambda s: (0,) * len(shape))
    scan_tile = lambda width: pl.BlockSpec((None, T, width), lambda s: (*scan_bt(s), 0))
    fin_tile = lambda width: pl.BlockSpec((None, T, width), lambda s: (*fin_bt(s), 0))
    state = lambda width: pl.BlockSpec((None, None, 256, width), lambda s: (*scan_bt(s), 0, 0))
    in_specs = [
        scan_tile(N_ACT), scan_tile(N_DEC), state(256), state(128), state(256), fin_tile(D),
        pl.BlockSpec((None, None, T, D_PLE), lambda s: (layer, *fin_bt(s), 0)),
        const((D, D)), const((D, D)), const((D_PLE, D)), const((1, D)), const((1, D)), const((1, D)),
        const((1, SSD_WIDTH)), const((1, 256)), const((1, 256)), const((1, 256)), const((1, SSD_WIDTH)),
        const((1, 256)),
        pl.BlockSpec(memory_space=pltpu.SMEM),
    ]
    scratch = [
        pltpu.VMEM((256, 256), F32), pltpu.VMEM((256, 128), F32), pltpu.VMEM((256, 256), F32),
        pltpu.VMEM((4, T, T), F32), pltpu.VMEM((T, D), BF16),
    ]
    return pl.pallas_call(
        functools.partial(_mix_body, tiles_per_seq=nT),
        grid=(n_tiles + 1,),
        in_specs=in_specs,
        out_specs=fin_tile(D),
        out_shape=jax.ShapeDtypeStruct((B, S, D), F32),
        scratch_shapes=scratch,
        compiler_params=pltpu.CompilerParams(
            dimension_semantics=("arbitrary",), vmem_limit_bytes=VMEM_LIMIT, flags=MIX_FLAGS),
        name="mix",
    )(acts, dec, sbs, sbg, sbr, h, p, wout, wpg, wpe, bpg, lnw, lnb, snw, gnw, rnw, rnb, dsk,
      lg_row, lg_vec)


def _permute_proj(w):
    pieces = [w[:, _Z0:_DT0], w[:, _GQ0:_GA0], w[:, _RQ0:_NIN], w[:, _DT0:_GQ0], w[:, _GA0:_RQ0],
              jnp.zeros((w.shape[0], N_PROJ - _NIN), w.dtype)]
    return jnp.concatenate(pieces, axis=1).astype(BF16)


def kernel(x, p, w_in, conv_w, conv_b, dt_bias, a_log, d_skip, ssd_norm_w, gla_w_a2, gla_b_a,
           gla_norm_w, ret_norm_w, ret_norm_b, w_out, ln_w, ln_b, w_pe, w_pg, b_pg):
    B, S, D = x.shape
    assert D == D_MODEL and S % TILE == 0 and TILE % GLA_CHUNK == 0
    half = RET_DK // 2
    inv = ROPE_BASE ** (-jnp.arange(half, dtype=F32) / half)
    cos_t, sin_t = _rope_table(jnp.tile(inv, 256 // half)[None, :], S)
    log_gamma = jnp.log(1.0 - 2.0 ** (-5.0 - jnp.arange(4, dtype=F32)))
    lg_row = jnp.repeat(log_gamma, 64)[None, :]
    pad112 = jnp.zeros((112,), F32)
    h = x
    for i in range(DEPTH):
        w_p = _permute_proj(w_in[i])
        dtb_row = jnp.concatenate([dt_bias[i].reshape(16), pad112])[None, :]
        alog_row = jnp.concatenate([a_log[i].reshape(16), pad112])[None, :]
        wsm = jnp.zeros((128, 256), F32)
        wsm = wsm.at[16:32, 0:128].set(gla_w_a2[i, 0]).at[32:48, 128:256].set(gla_w_a2[i, 1]).astype(BF16)
        acts, dec, sbs, sbg, sbr = _prep_call(
            h, w_p, conv_w[i], conv_b[i][None, :], dtb_row, alog_row, wsm,
            gla_b_a[i].reshape(1, 256), cos_t, sin_t, lg_row)
        h = _mix_call(
            acts, dec, sbs, sbg, sbr, h, p, i,
            w_out[i].astype(BF16), w_pg[i].astype(BF16), w_pe[i].astype(BF16), b_pg[i][None, :],
            ln_w[i][None, :], ln_b[i][None, :], ssd_norm_w[i][None, :],
            jnp.tile(gla_norm_w[i], 4)[None, :], ret_norm_w[i][None, :], ret_norm_b[i][None, :],
            jnp.repeat(d_skip[i], 64)[None, :], lg_row, log_gamma)
    return h
```

```python
import functools

import jax
import jax.numpy as jnp
from jax import lax
from jax.experimental import pallas as pl
from jax.experimental.pallas import tpu as pltpu

F32 = jnp.float32
BF16 = jnp.bfloat16

D_MODEL = 1024
D_PLE = 256
DEPTH = 2
SSD_HEADS = 8
SSD_WIDTH = 512
SSD_STATE = 128
SSD_CONV = 5
SSD_CONV_CH = 1024
GLA_DK = 32
GLA_TEMP = 16.0
RET_DK = 64
ROPE_BASE = 10000.0
DN_ALPHA = float((2 * DEPTH) ** 0.25)
LN_EPS = 1e-5
RMS_EPS = 1e-6
LOG2E = 1.4426950408889634

_Z0, _XBC0, _DT0, _GQ0, _GK0, _GV0, _GG0, _GA0, _RQ0, _RK0, _RV0, _RG0, _NIN = (
    0, 512, 1536, 1552, 1680, 1808, 2064, 2320, 2352, 2608, 2864, 3120, 3376)
N_PROJ = 3456
REST0 = 1536
N_ACT = 3328
N_DEC = 512

TILE = 256
HALO = 16
GLA_CHUNK = 64
VMEM_LIMIT = 56 * 1024 * 1024

PREP_ORDER = (
    "small", "x0", "x1", "c0", "x2", "c1", "x3", "c2", "z0", "c3", "z1", "rq_d", "rk_d", "rq_v", "rv_d",
    "rk_v", "gqk_d", "gv_d", "sb_out", "st_s", "gg_d", "st_g", "rg_d", "st_r")
MIX_ORDER = (
    "cum", "cg", "raw_s", "raw_r", "e1", "s_vec", "s_h0", "s_h1", "s_h2", "s_h3", "s_h4", "s_h5", "s_h6",
    "s_h7", "r_y", "ln", "e3", "s_state", "r_state", "out", "g_prep", "g_c0", "g_c1", "g_c2", "g_c3",
    "g_rec", "g_norm", "r_norm")


def _mm(a, b):
    return jnp.dot(a, b, preferred_element_type=F32)


def _mm_nt(a, b):
    return lax.dot_general(a, b, (((1,), (1,)), ((), ())), preferred_element_type=F32)


def _mm_tn(a, b):
    return lax.dot_general(a, b, (((0,), (0,)), ((), ())), preferred_element_type=F32)


def _iota(shape, dim):
    return lax.broadcasted_iota(jnp.int32, shape, dim)


def _split(x, parts):
    out, r = [], x
    for k in range(parts):
        piece = r.astype(BF16)
        out.append(piece)
        if k + 1 < parts:
            r = r - piece.astype(F32)
    return jnp.concatenate(out, axis=1)


def _fold(y, parts):
    w = y.shape[1] // parts
    acc = y[:, :w]
    for k in range(1, parts):
        acc = acc + y[:, k * w:(k + 1) * w]
    return acc


def _cumsum_rows(tri, x):
    return _fold(_mm(tri, _split(x, 3)), 3)


def _head_expander(first_lane, parts):
    r = _iota((128 * parts, SSD_WIDTH), 0) & 127
    c = _iota((128 * parts, SSD_WIDTH), 1) >> 6
    return jnp.where(r == c + first_lane, 1.0, 0.0).astype(BF16)


def _expand(blk, expander):
    return _mm(_split(blk, 2), expander)


def _sigmoid(x):
    return 1.0 / (1.0 + jnp.exp(-x))


def _softplus(x):
    return jnp.maximum(x, 0.0) + jnp.log1p(jnp.exp(-jnp.abs(x)))


def _tile_rows(x, n):
    return jnp.concatenate([x] * n, axis=0)


def _run_stages(stages, order):
    assert sorted(order) == sorted(stages), "every stage exactly once"
    for name in order:
        stages[name]()


def _rope_body(inv_ref, cos_ref, sin_ref):
    rows = cos_ref.shape[0]
    pos = (_iota((rows, 256), 0) + pl.program_id(0) * rows).astype(F32)
    ang = pos * inv_ref[...]
    first_half = (_iota((rows, 256), 1) & 63) < 32
    s = jnp.sin(ang)
    cos_ref[...] = jnp.cos(ang)
    sin_ref[...] = jnp.where(first_half, -s, s)


def _rope_table(inv_row, seq):
    rows = 512
    return pl.pallas_call(
        _rope_body,
        grid=(seq // rows,),
        in_specs=[pl.BlockSpec((1, 256), lambda i: (0, 0))],
        out_specs=[pl.BlockSpec((rows, 256), lambda i: (i, 0))] * 2,
        out_shape=[jax.ShapeDtypeStruct((seq, 256), F32)] * 2,
        name="rope_table",
    )(inv_row)


def _prep_body(hp_ref, h_ref, hn_ref, w_ref, cw_ref, cb_ref, dtb_ref, alog_ref, wsm_ref, ba_ref,
               cos_ref, sin_ref, lg_ref,
               acts_ref, dec_ref, sbs_ref, sbg_ref, sbr_ref,
               st_s, st_g, st_r, tri_ref, expb_ref, rtab_ref):
    T = TILE
    i = pl.program_id(1)
    n = pl.num_programs(1)
    t = n - 1 - i

    @pl.when(i == 0)
    def _():
        st_s[...] = jnp.zeros_like(st_s)
        st_g[...] = jnp.zeros_like(st_g)
        st_r[...] = jnp.zeros_like(st_r)

    @pl.when((pl.program_id(0) == 0) & (i == 0))
    def _():
        tri_ref[...] = jnp.where(_iota((T, T), 0) >= _iota((T, T), 1), 1.0, 0.0).astype(BF16)
        expb_ref[...] = _head_expander(SSD_HEADS, 2)
        rtab_ref[...] = jnp.exp(_iota((T, 256), 0).astype(F32) * lg_ref[...])

    hp = jnp.where(t > 0, hp_ref[...], 0.0)
    hn = jnp.where(t < n - 1, hn_ref[...], 0.0)
    hext = jnp.concatenate([hp, h_ref[...], hn], axis=0).astype(BF16)
    hm = hext[HALO:HALO + T]
    first_half = (_iota((T, 256), 1) & 63) < 32
    v = {}

    def proj(lhs, first, width=256):
        return _mm(lhs, w_ref[:, first:first + width])

    def rotary(u):
        swapped = jnp.where(first_half, pltpu.roll(u, 224, 1), pltpu.roll(u, 32, 1))
        return u * cos_ref[...] + swapped * sin_ref[...]

    def small():
        sm = proj(hm, REST0 + 1792, 128)
        lane128 = _iota((1, 128), 1)
        dt = _softplus(sm + dtb_ref[...])
        la_s = dt * jnp.where(lane128 < 2 * SSD_HEADS, -jnp.exp(alog_ref[...]), 0.0)
        pre = _mm(sm.astype(BF16), wsm_ref[...]) + ba_ref[...]
        la_g = -_softplus(-pre) * (1.0 / GLA_TEMP)
        dec_ref[:, 0:128] = dt
        dec_ref[:, 128:256] = la_s
        dec_ref[:, 256:512] = la_g
        lab = jnp.concatenate([la_s, la_g[:, 128:256]], axis=1)
        cum = _cumsum_rows(tri_ref[...], lab)
        v["dt"] = dt
        v["cbx"] = cum - lab
        v["tot"] = cum[T - 1:T, :]

    def x_piece(c):
        v["xs", c] = proj(hext, _XBC0 + 256 * c)

    def conv_piece(c):
        cols = slice(256 * c, 256 * c + 256)
        xs = v.pop(("xs", c))
        acc = jnp.broadcast_to(cb_ref[:, cols], (T, 256))
        for j in range(SSD_CONV):
            r0 = HALO - SSD_CONV // 2 + j
            acc = acc + cw_ref[j:j + 1, cols] * xs[r0:r0 + T, :]
        v["xbc", c] = acc * _sigmoid(acc)
        acts_ref[:, cols] = v["xbc", c].astype(BF16)

    def z_piece(c):
        acts_ref[:, 1024 + 256 * c:1280 + 256 * c] = proj(hm, 256 * c).astype(BF16)

    def rq_d():
        v["rq"] = proj(hm, REST0 + 768)

    def rq_v():
        acts_ref[:, 2304:2560] = rotary(v.pop("rq")).astype(BF16)

    def rk_d():
        v["rk_raw"] = proj(hm, REST0 + 1024)

    def rk_v():
        v["rk"] = rotary(v.pop("rk_raw")) * (RET_DK ** -0.5)
        acts_ref[:, 2560:2816] = v["rk"].astype(BF16)

    def rv_d():
        v["rv"] = proj(hm, REST0 + 1280).astype(BF16)
        acts_ref[:, 2816:3072] = v["rv"]

    def gqk_d():
        gqk = proj(hm, REST0)
        v["gk"] = gqk[:, 128:256]
        acts_ref[:, 1536:1664] = (gqk[:, 0:128] * (GLA_DK ** -0.5)).astype(BF16)
        acts_ref[:, 1664:1792] = v["gk"].astype(BF16)

    def gv_d():
        v["gv"] = proj(hm, REST0 + 256).astype(BF16)
        acts_ref[:, 1792:2048] = v["gv"]

    def gg_d():
        acts_ref[:, 2048:2304] = proj(hm, REST0 + 512).astype(BF16)

    def rg_d():
        acts_ref[:, 3072:3328] = proj(hm, REST0 + 1536).astype(BF16)

    def sb_out():
        sbs_ref[...] = st_s[...]
        sbg_ref[...] = st_g[...]
        sbr_ref[...] = st_r[...]

    def st_s_update():
        exp_b = expb_ref[...]
        wx = _expand(jnp.exp(v["cbx"][:, 0:128]) * v["dt"], exp_b)
        dec_row = _expand(jnp.broadcast_to(jnp.exp(v["tot"][:, 0:128]), (8, 128)), exp_b)[0:1]
        bm = v["xbc", 2].astype(BF16)
        for g in range(2):
            rows = slice(128 * g, 128 * g + 128)
            cols = slice(256 * g, 256 * g + 256)
            xw = (v["xbc", g] * wx[:, cols]).astype(BF16)
            st_s[rows, :] = st_s[rows, :] * dec_row[:, cols] + _mm_tn(bm[:, rows], xw)

    def st_g_update():
        mask_g = (_iota((256, 128), 0) >> 6) == (_iota((256, 128), 1) >> 5)
        kw = (v["gk"] * jnp.exp(v["cbx"][:, 128:256])).astype(BF16)
        st_g[...] = (st_g[...] * jnp.exp(v["tot"][:, 128:256])
                     + jnp.where(mask_g, _mm_tn(v["gv"], kw), 0.0))

    def st_r_update():
        mask_r = (_iota((256, 256), 0) >> 6) == (_iota((256, 256), 1) >> 6)
        kwr = (v["rk"] * rtab_ref[...]).astype(BF16)
        st_r[...] = (st_r[...] * jnp.exp(float(T) * lg_ref[...])
                     + jnp.where(mask_r, _mm_tn(v["rv"], kwr), 0.0))

    stages = {"small": small, "rq_d": rq_d, "rq_v": rq_v, "rk_d": rk_d, "rk_v": rk_v, "rv_d": rv_d,
              "gqk_d": gqk_d, "gv_d": gv_d, "gg_d": gg_d, "rg_d": rg_d, "sb_out": sb_out,
              "st_s": st_s_update, "st_g": st_g_update, "st_r": st_r_update}
    for c in range(4):
        stages[f"x{c}"] = functools.partial(x_piece, c)
        stages[f"c{c}"] = functools.partial(conv_piece, c)
    for c in range(2):
        stages[f"z{c}"] = functools.partial(z_piece, c)
    _run_stages(stages, PREP_ORDER)


def _prep_call(h, w_p, cw, cb, dtb_row, alog_row, wsm, ba_row, cos_t, sin_t, lg_row):
    B, S, D = h.shape
    T = TILE
    nT = S // T
    hb = T // HALO
    const = lambda shape: pl.BlockSpec(shape, lambda b, i: (0,) * len(shape))
    in_specs = [
        pl.BlockSpec((None, HALO, D), lambda b, i: (b, jnp.maximum((nT - 1 - i) * hb - 1, 0), 0)),
        pl.BlockSpec((None, T, D), lambda b, i: (b, nT - 1 - i, 0)),
        pl.BlockSpec((None, HALO, D), lambda b, i: (b, jnp.minimum((nT - i) * hb, S // HALO - 1), 0)),
        const((D, N_PROJ)), const((SSD_CONV, SSD_CONV_CH)), const((1, SSD_CONV_CH)),
        const((1, 128)), const((1, 128)), const((128, 256)), const((1, 256)),
        pl.BlockSpec((T, 256), lambda b, i: (nT - 1 - i, 0)),
        pl.BlockSpec((T, 256), lambda b, i: (nT - 1 - i, 0)),
        const((1, 256)),
    ]
    out_specs = [
        pl.BlockSpec((None, T, N_ACT), lambda b, i: (b, nT - 1 - i, 0)),
        pl.BlockSpec((None, T, N_DEC), lambda b, i: (b, nT - 1 - i, 0)),
        pl.BlockSpec((None, None, 256, 256), lambda b, i: (b, nT - 1 - i, 0, 0)),
        pl.BlockSpec((None, None, 256, 128), lambda b, i: (b, nT - 1 - i, 0, 0)),
        pl.BlockSpec((None, None, 256, 256), lambda b, i: (b, nT - 1 - i, 0, 0)),
    ]
    out_shape = [
        jax.ShapeDtypeStruct((B, S, N_ACT), BF16),
        jax.ShapeDtypeStruct((B, S, N_DEC), F32),
        jax.ShapeDtypeStruct((B, nT, 256, 256), F32),
        jax.ShapeDtypeStruct((B, nT, 256, 128), F32),
        jax.ShapeDtypeStruct((B, nT, 256, 256), F32),
    ]
    scratch = [pltpu.VMEM((256, 256), F32), pltpu.VMEM((256, 128), F32), pltpu.VMEM((256, 256), F32),
               pltpu.VMEM((T, T), BF16), pltpu.VMEM((256, SSD_WIDTH), BF16), pltpu.VMEM((T, 256), F32)]
    return pl.pallas_call(
        _prep_body,
        grid=(B, nT),
        in_specs=in_specs,
        out_specs=out_specs,
        out_shape=out_shape,
        scratch_shapes=scratch,
        compiler_params=pltpu.CompilerParams(
            dimension_semantics=("arbitrary", "arbitrary"), vmem_limit_bytes=VMEM_LIMIT),
        name="prep",
    )(h, h, h, w_p, cw, cb, dtb_row, alog_row, wsm, ba_row, cos_t, sin_t, lg_row)


def _group_mean(x, ones_blocks):
    return _mm(x.astype(BF16), ones_blocks) * (1.0 / 64.0)


def _mix_body(acts_ref, dec_ref, sbs_ref, sbg_ref, sbr_ref, h_ref, p_ref,
              wout_ref, wpg_ref, wpe_ref, bpg_ref, lnw_ref, lnb_ref,
              snw_ref, gnw_ref, rnw_ref, rnb_ref, dsk_ref, lg_ref, lgs_ref,
              out_ref, sf_s, sf_g, sf_r, dm_ref, ycat_ref,
              tri_ref, tric_ref, expf_ref, expb_ref, ones_ref, rtab_ref, *, tiles_per_seq):
    T = TILE
    H = T // 2
    L = GLA_CHUNK
    nc = T // L
    step = pl.program_id(0)
    scan_tile = jnp.minimum(step, pl.num_programs(0) - 2)

    @pl.when(lax.rem(scan_tile, tiles_per_seq) == 0)
    def _():
        sf_s[...] = jnp.zeros_like(sf_s)
        sf_g[...] = jnp.zeros_like(sf_g)
        sf_r[...] = jnp.zeros_like(sf_r)

    row = _iota((T, T), 0)
    col = _iota((T, T), 1)
    lower = row >= col
    mask_v = (_iota((256, 256), 0) >> 6) == (_iota((256, 256), 1) >> 6)

    @pl.when(step == 0)
    def _():
        ycat_ref[...] = jnp.zeros_like(ycat_ref)
        dist = jnp.abs(row - col).astype(F32)
        for hh in range(4):
            dm_ref[hh] = jnp.exp(dist * lgs_ref[hh])
        tri_ref[...] = jnp.where(lower, 1.0, 0.0).astype(BF16)
        tric_ref[...] = jnp.where(lower & ((row >> 6) == (col >> 6)), 1.0, 0.0).astype(BF16)
        expf_ref[...] = _head_expander(0, 2)
        expb_ref[...] = _head_expander(SSD_HEADS, 2)
        ones_ref[...] = jnp.where(mask_v, 1.0, 0.0).astype(BF16)
        tpos = _iota((T, 256), 0).astype(F32)
        rtab_ref[0] = jnp.exp((tpos + 1.0) * lg_ref[...])
        rtab_ref[1] = jnp.exp((float(T) - tpos) * lg_ref[...])
        rtab_ref[2] = jnp.exp((float(T - 1) - tpos) * lg_ref[...])

    mask_k = (_iota((256, 128), 0) >> 6) == (_iota((256, 128), 1) >> 5)
    mask_kk = (_iota((256, 256), 0) >> 6) == ((_iota((256, 256), 1) & 127) >> 5)
    lane_head = _iota((T, 256), 1) >> 6
    lower_h = lower[0:H, 0:H]
    lower_c = _iota((L, 256), 0) >= (_iota((L, 256), 1) & (L - 1))
    zero_bf = jnp.zeros((T, 256), BF16)
    group_rows = [slice(128 * g, 128 * g + 128) for g in range(2)]
    group_cols = [slice(256 * g, 256 * g + 256) for g in range(2)]
    x_bf = acts_ref[:, 0:512]
    b_bf = acts_ref[:, 512:768]
    c_bf = acts_ref[:, 768:1024]
    q_bf = acts_ref[:, 2304:2560]
    k_bf = acts_ref[:, 2560:2816]
    v_r = acts_ref[:, 2816:3072]
    v_bf = acts_ref[:, 1792:2048]
    v = {"y_g": [jnp.zeros((T, 256), F32), jnp.zeros((T, 256), F32)], "o_r": jnp.zeros((T, 256), F32),
         "y_c": [None] * nc, "a_c": [None] * nc}

    def cum():
        v["cum"] = _cumsum_rows(tri_ref[...], dec_ref[:, 128:256])

    def raw_s():
        v["raw_s"] = [_mm_nt(c_bf[:, group_rows[g]], b_bf[:, group_rows[g]]) for g in range(2)]

    def s_vec():
        dt = dec_ref[:, 0:128]
        cm = v["cum"]
        cbx = cm - dec_ref[:, 128:256]
        tot = cm[T - 1:T, :]
        ldt = jnp.log(dt)
        v["col_f"] = cm * LOG2E
        v["col_b"] = cbx * LOG2E
        v["row_f"] = ((cm - ldt) * LOG2E).T
        v["row_b"] = ((cbx + ldt) * LOG2E).T
        exp_f = expf_ref[...]
        v["e_f"] = _expand(jnp.exp(cm), exp_f)
        v["e_b"] = _expand(jnp.exp(tot - cbx), expb_ref[...])
        w_f = _expand(jnp.exp(tot - cm) * dt, exp_f)
        v["dec_f"] = _expand(jnp.broadcast_to(jnp.exp(tot), (8, 128)), exp_f)[0:1]
        v["x_f"] = x_bf.astype(F32)
        v["xw"] = (v["x_f"] * w_f).astype(BF16)

    def s_head(hd):
        g, hh = divmod(hd, 4)
        hb = SSD_HEADS + hd
        cf_h, cb_h = v["col_f"][:, hd:hd + 1], v["col_b"][:, hb:hb + 1]
        rf_h, rb_h = v["row_f"][hd:hd + 1, :], v["row_b"][hb:hb + 1, :]
        top = jnp.concatenate(
            [jnp.where(lower_h, cf_h[0:H] - rf_h[:, 0:H], rb_h[:, 0:H] - cb_h[0:H]),
             rb_h[:, H:T] - cb_h[0:H]], axis=1)
        bot = jnp.concatenate(
            [cf_h[H:T] - rf_h[:, 0:H],
             jnp.where(lower_h, cf_h[H:T] - rf_h[:, H:T], rb_h[:, H:T] - cb_h[H:T])], axis=1)
        m = jnp.exp2(jnp.concatenate([top, bot], axis=0))
        s = (v["raw_s"][g] * m).astype(BF16)
        x_g = x_bf[:, group_cols[g]]
        v["y_g"][g] = v["y_g"][g] + _mm(s, jnp.where(lane_head == hh, x_g, jnp.zeros_like(x_g)))

    def s_state():
        for g in range(2):
            rows, cols = group_rows[g], group_cols[g]
            st = jnp.concatenate([sf_s[rows, :], sbs_ref[rows, :]], axis=1).astype(BF16)
            cs = _mm(c_bf[:, rows], st)
            v["y_g"][g] = v["y_g"][g] + v["e_f"][:, cols] * cs[:, 0:256] + v["e_b"][:, cols] * cs[:, 256:512]
            sf_s[rows, :] = sf_s[rows, :] * v["dec_f"][:, cols] + _mm_tn(b_bf[:, rows], v["xw"][:, cols])
        y_ssd = jnp.concatenate(v["y_g"], axis=1) + dsk_ref[...] * v["x_f"]
        z = acts_ref[:, 1024:1536].astype(F32)
        yz = y_ssd * (z * _sigmoid(z))
        y1 = yz * lax.rsqrt(jnp.mean(yz * yz, axis=-1, keepdims=True) + RMS_EPS) * snw_ref[...]
        ycat_ref[:, 0:512] = y1.astype(BF16)

    def raw_r():
        v["raw_r"] = [_mm_nt(jnp.where(lane_head == hh, q_bf, zero_bf), k_bf) for hh in range(4)]

    def r_y():
        for hh in range(4):
            s = (v["raw_r"][hh] * dm_ref[hh]).astype(BF16)
            v["o_r"] = v["o_r"] + _mm(s, jnp.where(lane_head == hh, v_r, zero_bf))

    def r_state():
        q_f32 = q_bf.astype(F32)
        qq = jnp.concatenate([q_f32 * rtab_ref[0], q_f32 * rtab_ref[1]], axis=1).astype(BF16)
        ss = jnp.concatenate([sf_r[...], sbr_ref[...]], axis=1).astype(BF16)
        v["o_r"] = v["o_r"] + _mm_nt(qq, ss)
        kwf = (k_bf.astype(F32) * rtab_ref[2]).astype(BF16)
        sf_r[...] = sf_r[...] * jnp.exp(float(T) * lg_ref[...]) + jnp.where(mask_v, _mm_tn(v_r, kwf), 0.0)

    def r_norm():
        o_r = v["o_r"]
        mu = _group_mean(o_r, ones_ref[...])
        d = o_r - mu
        var = _group_mean(d * d, ones_ref[...])
        rg = acts_ref[:, 3072:3328].astype(F32)
        y3 = (d * lax.rsqrt(var + LN_EPS) * rnw_ref[...] + rnb_ref[...]) * (rg * _sigmoid(rg))
        ycat_ref[:, 768:1024] = y3.astype(BF16)

    def cg():
        v["cg"] = _cumsum_rows(tric_ref[...], dec_ref[:, 256:512])

    def g_prep():
        cgv = v["cg"]
        cf = cgv[:, 0:128]
        cbx_g = cgv[:, 128:256] - dec_ref[:, 384:512]
        v["last"] = [cgv[L * c + L - 1:L * c + L, :] for c in range(nc)]
        last_full = jnp.concatenate([jnp.broadcast_to(r, (L, 256)) for r in v["last"]], axis=0)
        q = acts_ref[:, 1536:1664].astype(F32)
        k = acts_ref[:, 1664:1792].astype(F32)
        v["qf"] = (q * jnp.exp(cf)).astype(BF16)
        v["kf"] = (k * jnp.exp(-cf)).astype(BF16)
        v["qb"] = (q * jnp.exp(-cbx_g)).astype(BF16)
        v["kb"] = (k * jnp.exp(cbx_g)).astype(BF16)
        v["ksf"] = (k * jnp.exp(last_full[:, 0:128] - cf)).astype(BF16)
        v["qbi"] = (q * jnp.exp(last_full[:, 128:256] - cbx_g)).astype(BF16)

    def g_chunk(c):
        sl = slice(L * c, L * c + L)
        kf_t = jnp.where(mask_k, _tile_rows(v["kf"][sl], 4), jnp.zeros((256, 128), BF16))
        kb_t = jnp.where(mask_k, _tile_rows(v["kb"][sl], 4), jnp.zeros((256, 128), BF16))
        sc = jnp.where(lower_c, _mm_nt(v["qf"][sl], kf_t), _mm_nt(v["qb"][sl], kb_t)).astype(BF16)
        vd = jnp.where(mask_v, _tile_rows(v_bf[sl], 4), jnp.zeros((256, 256), BF16))
        v["y_c"][c] = _mm(sc, vd)
        a = _mm_tn(v_bf[sl], jnp.concatenate([v["ksf"][sl], v["kb"][sl]], axis=1))
        v["a_c"][c] = jnp.where(mask_kk, a, 0.0)

    def g_rec():
        s_f, s_b = [None] * nc, [None] * nc
        s = sf_g[...]
        for c in range(nc):
            s_f[c] = s
            s = s * jnp.exp(v["last"][c][:, 0:128]) + v["a_c"][c][:, 0:128]
        sf_g[...] = s
        s = sbg_ref[...]
        for c in reversed(range(nc)):
            s_b[c] = s
            s = s * jnp.exp(v["last"][c][:, 128:256]) + v["a_c"][c][:, 128:256]
        for c in range(nc):
            sl = slice(L * c, L * c + L)
            qq = jnp.concatenate([v["qf"][sl], v["qbi"][sl]], axis=1)
            ss = jnp.concatenate([s_f[c], s_b[c]], axis=1).astype(BF16)
            v["y_c"][c] = v["y_c"][c] + _mm_nt(qq, ss)

    def g_norm():
        o_g = jnp.concatenate(v["y_c"], axis=0)
        gg = acts_ref[:, 2048:2304].astype(F32)
        ms = _group_mean(o_g * o_g, ones_ref[...])
        y2 = o_g * lax.rsqrt(ms + RMS_EPS) * gnw_ref[...] * (gg * _sigmoid(gg))
        ycat_ref[:, 512:768] = y2.astype(BF16)

    def e1():
        v["mix"] = _mm(ycat_ref[...], wout_ref[...])
        v["pe"] = _mm(p_ref[...].astype(BF16), wpe_ref[...])

    def ln():
        r = DN_ALPHA * h_ref[...] + v.pop("mix")
        mu_r = jnp.mean(r, axis=-1, keepdims=True)
        dr = r - mu_r
        var_r = jnp.mean(dr * dr, axis=-1, keepdims=True)
        v["hn"] = dr * lax.rsqrt(var_r + LN_EPS) * lnw_ref[...] + lnb_ref[...]

    def e3():
        v["gate_pre"] = _mm(v["hn"].astype(BF16), wpg_ref[...])

    def out():
        out_ref[...] = v["hn"] + _sigmoid(v["gate_pre"] + bpg_ref[...]) * v["pe"]

    stages = {"cum": cum, "cg": cg, "raw_s": raw_s, "raw_r": raw_r, "e1": e1, "s_vec": s_vec, "r_y": r_y,
              "ln": ln, "e3": e3, "s_state": s_state, "r_state": r_state, "out": out, "g_prep": g_prep,
              "g_rec": g_rec, "g_norm": g_norm, "r_norm": r_norm}
    for hd in range(SSD_HEADS):
        stages[f"s_h{hd}"] = functools.partial(s_head, hd)
    for c in range(nc):
        stages[f"g_c{c}"] = functools.partial(g_chunk, c)
    _run_stages(stages, MIX_ORDER)


def _mix_call(acts, dec, sbs, sbg, sbr, h, p, layer, wout, wpg, wpe, bpg, lnw, lnb,
              snw, gnw, rnw, rnb, dsk, lg_row, lg_vec):
    B, S, D = h.shape
    T = TILE
    nT = S // T
    n_tiles = B * nT
    scan_bt = lambda s: (lax.div(jnp.minimum(s, n_tiles - 1), nT), lax.rem(jnp.minimum(s, n_tiles - 1), nT))
    fin_bt = lambda s: (lax.div(jnp.maximum(s - 1, 0), nT), lax.rem(jnp.maximum(s - 1, 0), nT))
    const = lambda shape: pl.BlockSpec(shape, lambda s: (0,) * len(shape))
    scan_tile = lambda width: pl.BlockSpec((None, T, width), lambda s: (*scan_bt(s), 0))
    fin_tile = lambda width: pl.BlockSpec((None, T, width), lambda s: (*fin_bt(s), 0))
    state = lambda width: pl.BlockSpec((None, None, 256, width), lambda s: (*scan_bt(s), 0, 0))
    in_specs = [
        scan_tile(N_ACT), scan_tile(N_DEC), state(256), state(128), state(256), fin_tile(D),
        pl.BlockSpec((None, None, T, D_PLE), lambda s: (layer, *fin_bt(s), 0)),
        const((D, D)), const((D, D)), const((D_PLE, D)), const((1, D)), const((1, D)), const((1, D)),
        const((1, SSD_WIDTH)), const((1, 256)), const((1, 256)), const((1, 256)), const((1, SSD_WIDTH)),
        const((1, 256)),
        pl.BlockSpec(memory_space=pltpu.SMEM),
    ]
    scratch = [
        pltpu.VMEM((256, 256), F32), pltpu.VMEM((256, 128), F32), pltpu.VMEM((256, 256), F32),
        pltpu.VMEM((4, T, T), F32), pltpu.VMEM((T, D), BF16),
        pltpu.VMEM((T, T), BF16), pltpu.VMEM((T, T), BF16),
        pltpu.VMEM((256, SSD_WIDTH), BF16), pltpu.VMEM((256, SSD_WIDTH), BF16),
        pltpu.VMEM((256, 256), BF16), pltpu.VMEM((3, T, 256), F32),
    ]
    return pl.pallas_call(
        functools.partial(_mix_body, tiles_per_seq=nT),
        grid=(n_tiles + 1,),
        in_specs=in_specs,
        out_specs=fin_tile(D),
        out_shape=jax.ShapeDtypeStruct((B, S, D), F32),
        scratch_shapes=scratch,
        compiler_params=pltpu.CompilerParams(
            dimension_semantics=("arbitrary",), vmem_limit_bytes=VMEM_LIMIT),
        name="mix",
    )(acts, dec, sbs, sbg, sbr, h, p, wout, wpg, wpe, bpg, lnw, lnb, snw, gnw, rnw, rnb, dsk,
      lg_row, lg_vec)


def _permute_proj(w):
    pieces = [w[:, _Z0:_DT0], w[:, _GQ0:_GA0], w[:, _RQ0:_NIN], w[:, _DT0:_GQ0], w[:, _GA0:_RQ0],
              jnp.zeros((w.shape[0], N_PROJ - _NIN), w.dtype)]
    return jnp.concatenate(pieces, axis=1).astype(BF16)


def kernel(x, p, w_in, conv_w, conv_b, dt_bias, a_log, d_skip, ssd_norm_w, gla_w_a2, gla_b_a,
           gla_norm_w, ret_norm_w, ret_norm_b, w_out, ln_w, ln_b, w_pe, w_pg, b_pg):
    B, S, D = x.shape
    assert D == D_MODEL and S % TILE == 0 and TILE % GLA_CHUNK == 0
    half = RET_DK // 2
    inv = ROPE_BASE ** (-jnp.arange(half, dtype=F32) / half)
    cos_t, sin_t = _rope_table(jnp.tile(inv, 256 // half)[None, :], S)
    log_gamma = jnp.log(1.0 - 2.0 ** (-5.0 - jnp.arange(4, dtype=F32)))
    lg_row = jnp.repeat(log_gamma, 64)[None, :]
    pad112 = jnp.zeros((112,), F32)
    h = x
    for i in range(DEPTH):
        w_p = _permute_proj(w_in[i])
        dtb_row = jnp.concatenate([dt_bias[i].reshape(16), pad112])[None, :]
        alog_row = jnp.concatenate([a_log[i].reshape(16), pad112])[None, :]
        wsm = jnp.zeros((128, 256), F32)
        wsm = wsm.at[16:32, 0:128].set(gla_w_a2[i, 0]).at[32:48, 128:256].set(gla_w_a2[i, 1]).astype(BF16)
        acts, dec, sbs, sbg, sbr = _prep_call(
            h, w_p, conv_w[i], conv_b[i][None, :], dtb_row, alog_row, wsm,
            gla_b_a[i].reshape(1, 256), cos_t, sin_t, lg_row)
        h = _mix_call(
            acts, dec, sbs, sbg, sbr, h, p, i,
            w_out[i].astype(BF16), w_pg[i].astype(BF16), w_pe[i].astype(BF16), b_pg[i][None, :],
            ln_w[i][None, :], ln_b[i][None, :], ssd_norm_w[i][None, :],
            jnp.tile(gla_norm_w[i], 4)[None, :], ret_norm_w[i][None, :], ret_norm_b[i][None, :],
            jnp.repeat(d_skip[i], 64)[None, :], lg_row, log_gamma)
    return h
```

```python
import functools

import jax
import jax.numpy as jnp
from jax import lax
from jax.experimental import pallas as pl
from jax.experimental.pallas import tpu as pltpu

F32 = jnp.float32
BF16 = jnp.bfloat16

D_MODEL = 1024
D_PLE = 256
DEPTH = 2
SSD_HEADS = 8
SSD_WIDTH = 512
SSD_STATE = 128
SSD_CONV = 5
SSD_CONV_CH = 1024
GLA_DK = 32
GLA_TEMP = 16.0
RET_DK = 64
ROPE_BASE = 10000.0
DN_ALPHA = float((2 * DEPTH) ** 0.25)
LN_EPS = 1e-5
RMS_EPS = 1e-6
LOG2E = 1.4426950408889634

_Z0, _XBC0, _DT0, _GQ0, _GK0, _GV0, _GG0, _GA0, _RQ0, _RK0, _RV0, _RG0, _NIN = (
    0, 512, 1536, 1552, 1680, 1808, 2064, 2320, 2352, 2608, 2864, 3120, 3376)
N_PROJ = 3456
REST0 = 1536
N_ACT = 3328
N_DEC = 512

TILE = 256
HALO = 16
GLA_CHUNK = 64
VMEM_LIMIT = 56 * 1024 * 1024

PREP_ORDER = (
    "x0", "x1", "small", "x3", "c0", "c1", "x2", "rq_d", "c2", "z0", "c3", "z1", "gqk_d", "rk_d", "rv_d",
    "rk_v", "rq_v", "sb_out", "gv_d", "gg_d", "st_g", "st_r", "rg_d", "st_s")
MIX_ORDER = (
    "cum", "cg", "raw_r", "raw_s", "e1", "pe", "s_vec", "s_exp", "s_h3", "s_h1", "s_h0", "s_h7", "s_h6",
    "s_h4", "r_y0", "r_y1", "r_y2", "r_y3", "s_h5", "ln", "e3", "r_inter", "r_upd", "s_h2", "out", "g_prep",
    "g_s3", "g_a3", "g_s1", "g_a1", "g_s2", "g_a2", "g_s0", "g_a0", "s_st0", "s_st1", "s_fin", "g_rec",
    "g_i0", "g_i1", "g_i2", "g_i3", "g_norm", "r_norm")


def _mm(a, b):
    return jnp.dot(a, b, preferred_element_type=F32)


def _mm_nt(a, b):
    return lax.dot_general(a, b, (((1,), (1,)), ((), ())), preferred_element_type=F32)


def _mm_tn(a, b):
    return lax.dot_general(a, b, (((0,), (0,)), ((), ())), preferred_element_type=F32)


def _iota(shape, dim):
    return lax.broadcasted_iota(jnp.int32, shape, dim)


def _split(x, parts):
    out, r = [], x
    for k in range(parts):
        piece = r.astype(BF16)
        out.append(piece)
        if k + 1 < parts:
            r = r - piece.astype(F32)
    return jnp.concatenate(out, axis=1)


def _fold(y, parts):
    w = y.shape[1] // parts
    acc = y[:, :w]
    for k in range(1, parts):
        acc = acc + y[:, k * w:(k + 1) * w]
    return acc


def _cumsum_rows(tri, x):
    return _fold(_mm(tri, _split(x, 3)), 3)


def _head_expander(first_lane, parts):
    r = _iota((128 * parts, SSD_WIDTH), 0) & 127
    c = _iota((128 * parts, SSD_WIDTH), 1) >> 6
    return jnp.where(r == c + first_lane, 1.0, 0.0).astype(BF16)


def _expand(blk, expander):
    return _mm(_split(blk, 2), expander)


def _sigmoid(x):
    return 1.0 / (1.0 + jnp.exp(-x))


def _softplus(x):
    return jnp.maximum(x, 0.0) + jnp.log1p(jnp.exp(-jnp.abs(x)))


def _tile_rows(x, n):
    return jnp.concatenate([x] * n, axis=0)


def _run_stages(stages, order):
    assert sorted(order) == sorted(stages), "every stage exactly once"
    for name in order:
        stages[name]()


def _rope_body(inv_ref, cos_ref, sin_ref):
    rows = cos_ref.shape[0]
    pos = (_iota((rows, 256), 0) + pl.program_id(0) * rows).astype(F32)
    ang = pos * inv_ref[...]
    first_half = (_iota((rows, 256), 1) & 63) < 32
    s = jnp.sin(ang)
    cos_ref[...] = jnp.cos(ang)
    sin_ref[...] = jnp.where(first_half, -s, s)


def _rope_table(inv_row, seq):
    rows = 512
    return pl.pallas_call(
        _rope_body,
        grid=(seq // rows,),
        in_specs=[pl.BlockSpec((1, 256), lambda i: (0, 0))],
        out_specs=[pl.BlockSpec((rows, 256), lambda i: (i, 0))] * 2,
        out_shape=[jax.ShapeDtypeStruct((seq, 256), F32)] * 2,
        name="rope_table",
    )(inv_row)


def _prep_body(hp_ref, h_ref, hn_ref, w_ref, cw_ref, cb_ref, dtb_ref, alog_ref, wsm_ref, ba_ref,
               cos_ref, sin_ref, lg_ref,
               acts_ref, dec_ref, sbs_ref, sbg_ref, sbr_ref,
               st_s, st_g, st_r, tri_ref, expb_ref, rtab_ref):
    T = TILE
    i = pl.program_id(1)
    n = pl.num_programs(1)
    t = n - 1 - i

    @pl.when(i == 0)
    def _():
        st_s[...] = jnp.zeros_like(st_s)
        st_g[...] = jnp.zeros_like(st_g)
        st_r[...] = jnp.zeros_like(st_r)

    @pl.when((pl.program_id(0) == 0) & (i == 0))
    def _():
        tri_ref[...] = jnp.where(_iota((T, T), 0) >= _iota((T, T), 1), 1.0, 0.0).astype(BF16)
        expb_ref[...] = _head_expander(SSD_HEADS, 2)
        rtab_ref[...] = jnp.exp(_iota((T, 256), 0).astype(F32) * lg_ref[...])

    hp = jnp.where(t > 0, hp_ref[...], 0.0)
    hn = jnp.where(t < n - 1, hn_ref[...], 0.0)
    hext = jnp.concatenate([hp, h_ref[...], hn], axis=0).astype(BF16)
    hm = hext[HALO:HALO + T]
    first_half = (_iota((T, 256), 1) & 63) < 32
    v = {}

    def proj(lhs, first, width=256):
        return _mm(lhs, w_ref[:, first:first + width])

    def rotary(u):
        swapped = jnp.where(first_half, pltpu.roll(u, 224, 1), pltpu.roll(u, 32, 1))
        return u * cos_ref[...] + swapped * sin_ref[...]

    def small():
        sm = proj(hm, REST0 + 1792, 128)
        lane128 = _iota((1, 128), 1)
        dt = _softplus(sm + dtb_ref[...])
        la_s = dt * jnp.where(lane128 < 2 * SSD_HEADS, -jnp.exp(alog_ref[...]), 0.0)
        pre = _mm(sm.astype(BF16), wsm_ref[...]) + ba_ref[...]
        la_g = -_softplus(-pre) * (1.0 / GLA_TEMP)
        dec_ref[:, 0:128] = dt
        dec_ref[:, 128:256] = la_s
        dec_ref[:, 256:512] = la_g
        lab = jnp.concatenate([la_s, la_g[:, 128:256]], axis=1)
        cum = _cumsum_rows(tri_ref[...], lab)
        v["dt"] = dt
        v["cbx"] = cum - lab
        v["tot"] = cum[T - 1:T, :]

    def x_piece(c):
        v["xs", c] = proj(hext, _XBC0 + 256 * c)

    def conv_piece(c):
        cols = slice(256 * c, 256 * c + 256)
        xs = v.pop(("xs", c))
        acc = jnp.broadcast_to(cb_ref[:, cols], (T, 256))
        for j in range(SSD_CONV):
            r0 = HALO - SSD_CONV // 2 + j
            acc = acc + cw_ref[j:j + 1, cols] * xs[r0:r0 + T, :]
        v["xbc", c] = acc * _sigmoid(acc)
        acts_ref[:, cols] = v["xbc", c].astype(BF16)

    def z_piece(c):
        acts_ref[:, 1024 + 256 * c:1280 + 256 * c] = proj(hm, 256 * c).astype(BF16)

    def rq_d():
        v["rq"] = proj(hm, REST0 + 768)

    def rq_v():
        acts_ref[:, 2304:2560] = rotary(v.pop("rq")).astype(BF16)

    def rk_d():
        v["rk_raw"] = proj(hm, REST0 + 1024)

    def rk_v():
        v["rk"] = rotary(v.pop("rk_raw")) * (RET_DK ** -0.5)
        acts_ref[:, 2560:2816] = v["rk"].astype(BF16)

    def rv_d():
        v["rv"] = proj(hm, REST0 + 1280).astype(BF16)
        acts_ref[:, 2816:3072] = v["rv"]

    def gqk_d():
        gqk = proj(hm, REST0)
        v["gk"] = gqk[:, 128:256]
        acts_ref[:, 1536:1664] = (gqk[:, 0:128] * (GLA_DK ** -0.5)).astype(BF16)
        acts_ref[:, 1664:1792] = v["gk"].astype(BF16)

    def gv_d():
        v["gv"] = proj(hm, REST0 + 256).astype(BF16)
        acts_ref[:, 1792:2048] = v["gv"]

    def gg_d():
        acts_ref[:, 2048:2304] = proj(hm, REST0 + 512).astype(BF16)

    def rg_d():
        acts_ref[:, 3072:3328] = proj(hm, REST0 + 1536).astype(BF16)

    def sb_out():
        sbs_ref[...] = st_s[...]
        sbg_ref[...] = st_g[...]
        sbr_ref[...] = st_r[...]

    def st_s_update():
        exp_b = expb_ref[...]
        wx = _expand(jnp.exp(v["cbx"][:, 0:128]) * v["dt"], exp_b)
        dec_row = _expand(jnp.broadcast_to(jnp.exp(v["tot"][:, 0:128]), (8, 128)), exp_b)[0:1]
        bm = v["xbc", 2].astype(BF16)
        for g in range(2):
            rows = slice(128 * g, 128 * g + 128)
            cols = slice(256 * g, 256 * g + 256)
            xw = (v["xbc", g] * wx[:, cols]).astype(BF16)
            st_s[rows, :] = st_s[rows, :] * dec_row[:, cols] + _mm_tn(bm[:, rows], xw)

    def st_g_update():
        mask_g = (_iota((256, 128), 0) >> 6) == (_iota((256, 128), 1) >> 5)
        kw = (v["gk"] * jnp.exp(v["cbx"][:, 128:256])).astype(BF16)
        st_g[...] = (st_g[...] * jnp.exp(v["tot"][:, 128:256])
                     + jnp.where(mask_g, _mm_tn(v["gv"], kw), 0.0))

    def st_r_update():
        mask_r = (_iota((256, 256), 0) >> 6) == (_iota((256, 256), 1) >> 6)
        kwr = (v["rk"] * rtab_ref[...]).astype(BF16)
        st_r[...] = (st_r[...] * jnp.exp(float(T) * lg_ref[...])
                     + jnp.where(mask_r, _mm_tn(v["rv"], kwr), 0.0))

    stages = {"small": small, "rq_d": rq_d, "rq_v": rq_v, "rk_d": rk_d, "rk_v": rk_v, "rv_d": rv_d,
              "gqk_d": gqk_d, "gv_d": gv_d, "gg_d": gg_d, "rg_d": rg_d, "sb_out": sb_out,
              "st_s": st_s_update, "st_g": st_g_update, "st_r": st_r_update}
    for c in range(4):
        stages[f"x{c}"] = functools.partial(x_piece, c)
        stages[f"c{c}"] = functools.partial(conv_piece, c)
    for c in range(2):
        stages[f"z{c}"] = functools.partial(z_piece, c)
    _run_stages(stages, PREP_ORDER)


def _prep_call(h, w_p, cw, cb, dtb_row, alog_row, wsm, ba_row, cos_t, sin_t, lg_row):
    B, S, D = h.shape
    T = TILE
    nT = S // T
    hb = T // HALO
    const = lambda shape: pl.BlockSpec(shape, lambda b, i: (0,) * len(shape))
    in_specs = [
        pl.BlockSpec((None, HALO, D), lambda b, i: (b, jnp.maximum((nT - 1 - i) * hb - 1, 0), 0)),
        pl.BlockSpec((None, T, D), lambda b, i: (b, nT - 1 - i, 0)),
        pl.BlockSpec((None, HALO, D), lambda b, i: (b, jnp.minimum((nT - i) * hb, S // HALO - 1), 0)),
        const((D, N_PROJ)), const((SSD_CONV, SSD_CONV_CH)), const((1, SSD_CONV_CH)),
        const((1, 128)), const((1, 128)), const((128, 256)), const((1, 256)),
        pl.BlockSpec((T, 256), lambda b, i: (nT - 1 - i, 0)),
        pl.BlockSpec((T, 256), lambda b, i: (nT - 1 - i, 0)),
        const((1, 256)),
    ]
    out_specs = [
        pl.BlockSpec((None, T, N_ACT), lambda b, i: (b, nT - 1 - i, 0)),
        pl.BlockSpec((None, T, N_DEC), lambda b, i: (b, nT - 1 - i, 0)),
        pl.BlockSpec((None, None, 256, 256), lambda b, i: (b, nT - 1 - i, 0, 0)),
        pl.BlockSpec((None, None, 256, 128), lambda b, i: (b, nT - 1 - i, 0, 0)),
        pl.BlockSpec((None, None, 256, 256), lambda b, i: (b, nT - 1 - i, 0, 0)),
    ]
    out_shape = [
        jax.ShapeDtypeStruct((B, S, N_ACT), BF16),
        jax.ShapeDtypeStruct((B, S, N_DEC), F32),
        jax.ShapeDtypeStruct((B, nT, 256, 256), F32),
        jax.ShapeDtypeStruct((B, nT, 256, 128), F32),
        jax.ShapeDtypeStruct((B, nT, 256, 256), F32),
    ]
    scratch = [pltpu.VMEM((256, 256), F32), pltpu.VMEM((256, 128), F32), pltpu.VMEM((256, 256), F32),
               pltpu.VMEM((T, T), BF16), pltpu.VMEM((256, SSD_WIDTH), BF16), pltpu.VMEM((T, 256), F32)]
    return pl.pallas_call(
        _prep_body,
        grid=(B, nT),
        in_specs=in_specs,
        out_specs=out_specs,
        out_shape=out_shape,
        scratch_shapes=scratch,
        compiler_params=pltpu.CompilerParams(
            dimension_semantics=("arbitrary", "arbitrary"), vmem_limit_bytes=VMEM_LIMIT),
        name="prep",
    )(h, h, h, w_p, cw, cb, dtb_row, alog_row, wsm, ba_row, cos_t, sin_t, lg_row)


def _group_mean(x, ones_blocks):
    return _mm(x.astype(BF16), ones_blocks) * (1.0 / 64.0)


def _mix_body(acts_ref, dec_ref, sbs_ref, sbg_ref, sbr_ref, h_ref, p_ref,
              wout_ref, wpg_ref, wpe_ref, bpg_ref, lnw_ref, lnb_ref,
              snw_ref, gnw_ref, rnw_ref, rnb_ref, dsk_ref, lg_ref, lgs_ref,
              out_ref, sf_s, sf_g, sf_r, dm_ref, ycat_ref,
              tri_ref, tric_ref, expf_ref, expb_ref, ones_ref, rtab_ref, *, tiles_per_seq):
    T = TILE
    H = T // 2
    L = GLA_CHUNK
    nc = T // L
    step = pl.program_id(0)
    scan_tile = jnp.minimum(step, pl.num_programs(0) - 2)

    @pl.when(lax.rem(scan_tile, tiles_per_seq) == 0)
    def _():
        sf_s[...] = jnp.zeros_like(sf_s)
        sf_g[...] = jnp.zeros_like(sf_g)
        sf_r[...] = jnp.zeros_like(sf_r)

    row = _iota((T, T), 0)
    col = _iota((T, T), 1)
    lower = row >= col
    mask_v = (_iota((256, 256), 0) >> 6) == (_iota((256, 256), 1) >> 6)

    @pl.when(step == 0)
    def _():
        ycat_ref[...] = jnp.zeros_like(ycat_ref)
        dist = jnp.abs(row - col).astype(F32)
        for hh in range(4):
            dm_ref[hh] = jnp.exp(dist * lgs_ref[hh])
        tri_ref[...] = jnp.where(lower, 1.0, 0.0).astype(BF16)
        tric_ref[...] = jnp.where(lower & ((row >> 6) == (col >> 6)), 1.0, 0.0).astype(BF16)
        expf_ref[...] = _head_expander(0, 2)
        expb_ref[...] = _head_expander(SSD_HEADS, 2)
        ones_ref[...] = jnp.where(mask_v, 1.0, 0.0).astype(BF16)
        tpos = _iota((T, 256), 0).astype(F32)
        rtab_ref[0] = jnp.exp((tpos + 1.0) * lg_ref[...])
        rtab_ref[1] = jnp.exp((float(T) - tpos) * lg_ref[...])
        rtab_ref[2] = jnp.exp((float(T - 1) - tpos) * lg_ref[...])

    mask_k = (_iota((256, 128), 0) >> 6) == (_iota((256, 128), 1) >> 5)
    mask_kk = (_iota((256, 256), 0) >> 6) == ((_iota((256, 256), 1) & 127) >> 5)
    lane_head = _iota((T, 256), 1) >> 6
    lower_h = lower[0:H, 0:H]
    lower_c = _iota((L, 256), 0) >= (_iota((L, 256), 1) & (L - 1))
    zero_bf = jnp.zeros((T, 256), BF16)
    group_rows = [slice(128 * g, 128 * g + 128) for g in range(2)]
    group_cols = [slice(256 * g, 256 * g + 256) for g in range(2)]
    x_bf = acts_ref[:, 0:512]
    b_bf = acts_ref[:, 512:768]
    c_bf = acts_ref[:, 768:1024]
    q_bf = acts_ref[:, 2304:2560]
    k_bf = acts_ref[:, 2560:2816]
    v_r = acts_ref[:, 2816:3072]
    v_bf = acts_ref[:, 1792:2048]
    v = {"y_g": [jnp.zeros((T, 256), F32), jnp.zeros((T, 256), F32)], "o_r": jnp.zeros((T, 256), F32),
         "y_c": [None] * nc, "a_c": [None] * nc}

    def cum():
        v["cum"] = _cumsum_rows(tri_ref[...], dec_ref[:, 128:256])

    def raw_s():
        v["raw_s"] = [_mm_nt(c_bf[:, group_rows[g]], b_bf[:, group_rows[g]]) for g in range(2)]

    def s_vec():
        cm = v["cum"]
        cbx = cm - dec_ref[:, 128:256]
        ldt = jnp.log(dec_ref[:, 0:128])
        v["col_f"] = cm * LOG2E
        v["col_b"] = cbx * LOG2E
        v["row_f"] = ((cm - ldt) * LOG2E).T
        v["row_b"] = ((cbx + ldt) * LOG2E).T

    def s_exp():
        dt = dec_ref[:, 0:128]
        cm = v["cum"]
        cbx = cm - dec_ref[:, 128:256]
        tot = cm[T - 1:T, :]
        exp_f = expf_ref[...]
        v["e_f"] = _expand(jnp.exp(cm), exp_f)
        v["e_b"] = _expand(jnp.exp(tot - cbx), expb_ref[...])
        w_f = _expand(jnp.exp(tot - cm) * dt, exp_f)
        v["dec_f"] = _expand(jnp.broadcast_to(jnp.exp(tot), (8, 128)), exp_f)[0:1]
        v["x_f"] = x_bf.astype(F32)
        v["xw"] = (v["x_f"] * w_f).astype(BF16)

    def s_head(hd):
        g, hh = divmod(hd, 4)
        hb = SSD_HEADS + hd
        cf_h, cb_h = v["col_f"][:, hd:hd + 1], v["col_b"][:, hb:hb + 1]
        rf_h, rb_h = v["row_f"][hd:hd + 1, :], v["row_b"][hb:hb + 1, :]
        top = jnp.concatenate(
            [jnp.where(lower_h, cf_h[0:H] - rf_h[:, 0:H], rb_h[:, 0:H] - cb_h[0:H]),
             rb_h[:, H:T] - cb_h[0:H]], axis=1)
        bot = jnp.concatenate(
            [cf_h[H:T] - rf_h[:, 0:H],
             jnp.where(lower_h, cf_h[H:T] - rf_h[:, H:T], rb_h[:, H:T] - cb_h[H:T])], axis=1)
        m = jnp.exp2(jnp.concatenate([top, bot], axis=0))
        s = (v["raw_s"][g] * m).astype(BF16)
        x_g = x_bf[:, group_cols[g]]
        v["y_g"][g] = v["y_g"][g] + _mm(s, jnp.where(lane_head == hh, x_g, jnp.zeros_like(x_g)))

    def s_state(g):
        rows, cols = group_rows[g], group_cols[g]
        st = jnp.concatenate([sf_s[rows, :], sbs_ref[rows, :]], axis=1).astype(BF16)
        cs = _mm(c_bf[:, rows], st)
        v["y_g"][g] = v["y_g"][g] + v["e_f"][:, cols] * cs[:, 0:256] + v["e_b"][:, cols] * cs[:, 256:512]
        sf_s[rows, :] = sf_s[rows, :] * v["dec_f"][:, cols] + _mm_tn(b_bf[:, rows], v["xw"][:, cols])

    def s_fin():
        y_ssd = jnp.concatenate(v["y_g"], axis=1) + dsk_ref[...] * v["x_f"]
        z = acts_ref[:, 1024:1536].astype(F32)
        yz = y_ssd * (z * _sigmoid(z))
        y1 = yz * lax.rsqrt(jnp.mean(yz * yz, axis=-1, keepdims=True) + RMS_EPS) * snw_ref[...]
        ycat_ref[:, 0:512] = y1.astype(BF16)

    def raw_r():
        v["raw_r"] = [_mm_nt(jnp.where(lane_head == hh, q_bf, zero_bf), k_bf) for hh in range(4)]

    def r_y(hh):
        s = (v["raw_r"][hh] * dm_ref[hh]).astype(BF16)
        v["o_r"] = v["o_r"] + _mm(s, jnp.where(lane_head == hh, v_r, zero_bf))

    def r_inter():
        q_f32 = q_bf.astype(F32)
        qq = jnp.concatenate([q_f32 * rtab_ref[0], q_f32 * rtab_ref[1]], axis=1).astype(BF16)
        ss = jnp.concatenate([sf_r[...], sbr_ref[...]], axis=1).astype(BF16)
        v["o_r"] = v["o_r"] + _mm_nt(qq, ss)

    def r_upd():
        kwf = (k_bf.astype(F32) * rtab_ref[2]).astype(BF16)
        sf_r[...] = sf_r[...] * jnp.exp(float(T) * lg_ref[...]) + jnp.where(mask_v, _mm_tn(v_r, kwf), 0.0)

    def r_norm():
        o_r = v["o_r"]
        mu = _group_mean(o_r, ones_ref[...])
        d = o_r - mu
        var = _group_mean(d * d, ones_ref[...])
        rg = acts_ref[:, 3072:3328].astype(F32)
        y3 = (d * lax.rsqrt(var + LN_EPS) * rnw_ref[...] + rnb_ref[...]) * (rg * _sigmoid(rg))
        ycat_ref[:, 768:1024] = y3.astype(BF16)

    def cg():
        v["cg"] = _cumsum_rows(tric_ref[...], dec_ref[:, 256:512])

    def g_prep():
        cgv = v["cg"]
        cf = cgv[:, 0:128]
        cbx_g = cgv[:, 128:256] - dec_ref[:, 384:512]
        v["last"] = [cgv[L * c + L - 1:L * c + L, :] for c in range(nc)]
        last_full = jnp.concatenate([jnp.broadcast_to(r, (L, 256)) for r in v["last"]], axis=0)
        q = acts_ref[:, 1536:1664].astype(F32)
        k = acts_ref[:, 1664:1792].astype(F32)
        v["qf"] = (q * jnp.exp(cf)).astype(BF16)
        v["kf"] = (k * jnp.exp(-cf)).astype(BF16)
        v["qb"] = (q * jnp.exp(-cbx_g)).astype(BF16)
        v["kb"] = (k * jnp.exp(cbx_g)).astype(BF16)
        v["ksf"] = (k * jnp.exp(last_full[:, 0:128] - cf)).astype(BF16)
        v["qbi"] = (q * jnp.exp(last_full[:, 128:256] - cbx_g)).astype(BF16)

    def g_score(c):
        sl = slice(L * c, L * c + L)
        kf_t = jnp.where(mask_k, _tile_rows(v["kf"][sl], 4), jnp.zeros((256, 128), BF16))
        kb_t = jnp.where(mask_k, _tile_rows(v["kb"][sl], 4), jnp.zeros((256, 128), BF16))
        sc = jnp.where(lower_c, _mm_nt(v["qf"][sl], kf_t), _mm_nt(v["qb"][sl], kb_t)).astype(BF16)
        vd = jnp.where(mask_v, _tile_rows(v_bf[sl], 4), jnp.zeros((256, 256), BF16))
        v["y_c"][c] = _mm(sc, vd)

    def g_contrib(c):
        sl = slice(L * c, L * c + L)
        a = _mm_tn(v_bf[sl], jnp.concatenate([v["ksf"][sl], v["kb"][sl]], axis=1))
        v["a_c"][c] = jnp.where(mask_kk, a, 0.0)

    def g_rec():
        s_f, s_b = [None] * nc, [None] * nc
        s = sf_g[...]
        for c in range(nc):
            s_f[c] = s
            s = s * jnp.exp(v["last"][c][:, 0:128]) + v["a_c"][c][:, 0:128]
        sf_g[...] = s
        s = sbg_ref[...]
        for c in reversed(range(nc)):
            s_b[c] = s
            s = s * jnp.exp(v["last"][c][:, 128:256]) + v["a_c"][c][:, 128:256]
        v["ss_c"] = [jnp.concatenate([s_f[c], s_b[c]], axis=1).astype(BF16) for c in range(nc)]

    def g_inter(c):
        sl = slice(L * c, L * c + L)
        qq = jnp.concatenate([v["qf"][sl], v["qbi"][sl]], axis=1)
        v["y_c"][c] = v["y_c"][c] + _mm_nt(qq, v["ss_c"][c])

    def g_norm():
        o_g = jnp.concatenate(v["y_c"], axis=0)
        gg = acts_ref[:, 2048:2304].astype(F32)
        ms = _group_mean(o_g * o_g, ones_ref[...])
        y2 = o_g * lax.rsqrt(ms + RMS_EPS) * gnw_ref[...] * (gg * _sigmoid(gg))
        ycat_ref[:, 512:768] = y2.astype(BF16)

    def by_columns(lhs, w_ref):
        return jnp.concatenate([_mm(lhs, w_ref[:, c:c + 256]) for c in range(0, w_ref.shape[1], 256)], axis=1)

    def e1():
        v["mix"] = by_columns(ycat_ref[...], wout_ref)

    def pe():
        v["pe"] = by_columns(p_ref[...].astype(BF16), wpe_ref)

    def ln():
        r = DN_ALPHA * h_ref[...] + v.pop("mix")
        mu_r = jnp.mean(r, axis=-1, keepdims=True)
        dr = r - mu_r
        var_r = jnp.mean(dr * dr, axis=-1, keepdims=True)
        v["hn"] = dr * lax.rsqrt(var_r + LN_EPS) * lnw_ref[...] + lnb_ref[...]

    def e3():
        v["gate_pre"] = by_columns(v["hn"].astype(BF16), wpg_ref)

    def out():
        out_ref[...] = v["hn"] + _sigmoid(v["gate_pre"] + bpg_ref[...]) * v["pe"]

    stages = {"cum": cum, "cg": cg, "raw_s": raw_s, "raw_r": raw_r, "e1": e1, "pe": pe, "s_vec": s_vec,
              "s_exp": s_exp, "ln": ln, "e3": e3, "s_fin": s_fin, "r_inter": r_inter, "r_upd": r_upd,
              "out": out, "g_prep": g_prep, "g_rec": g_rec, "g_norm": g_norm, "r_norm": r_norm}
    for hd in range(SSD_HEADS):
        stages[f"s_h{hd}"] = functools.partial(s_head, hd)
    for g in range(2):
        stages[f"s_st{g}"] = functools.partial(s_state, g)
    for hh in range(4):
        stages[f"r_y{hh}"] = functools.partial(r_y, hh)
    for c in range(nc):
        stages[f"g_s{c}"] = functools.partial(g_score, c)
        stages[f"g_a{c}"] = functools.partial(g_contrib, c)
        stages[f"g_i{c}"] = functools.partial(g_inter, c)
    _run_stages(stages, MIX_ORDER)


def _mix_call(acts, dec, sbs, sbg, sbr, h, p, layer, wout, wpg, wpe, bpg, lnw, lnb,
              snw, gnw, rnw, rnb, dsk, lg_row, lg_vec):
    B, S, D = h.shape
    T = TILE
    nT = S // T
    n_tiles = B * nT
    scan_bt = lambda s: (lax.div(jnp.minimum(s, n_tiles - 1), nT), lax.rem(jnp.minimum(s, n_tiles - 1), nT))
    fin_bt = lambda s: (lax.div(jnp.maximum(s - 1, 0), nT), lax.rem(jnp.maximum(s - 1, 0), nT))
    const = lambda shape: pl.BlockSpec(shape, lambda s: (0,) * len(shape))
    scan_tile = lambda width: pl.BlockSpec((None, T, width), lambda s: (*scan_bt(s), 0))
    fin_tile = lambda width: pl.BlockSpec((None, T, width), lambda s: (*fin_bt(s), 0))
    state = lambda width: pl.BlockSpec((None, None, 256, width), lambda s: (*scan_bt(s), 0, 0))
    in_specs = [
        scan_tile(N_ACT), scan_tile(N_DEC), state(256), state(128), state(256), fin_tile(D),
        pl.BlockSpec((None, None, T, D_PLE), lambda s: (layer, *fin_bt(s), 0)),
        const((D, D)), const((D, D)), const((D_PLE, D)), const((1, D)), const((1, D)), const((1, D)),
        const((1, SSD_WIDTH)), const((1, 256)), const((1, 256)), const((1, 256)), const((1, SSD_WIDTH)),
        const((1, 256)),
        pl.BlockSpec(memory_space=pltpu.SMEM),
    ]
    scratch = [
        pltpu.VMEM((256, 256), F32), pltpu.VMEM((256, 128), F32), pltpu.VMEM((256, 256), F32),
        pltpu.VMEM((4, T, T), F32), pltpu.VMEM((T, D), BF16),
        pltpu.VMEM((T, T), BF16), pltpu.VMEM((T, T), BF16),
        pltpu.VMEM((256, SSD_WIDTH), BF16), pltpu.VMEM((256, SSD_WIDTH), BF16),
        pltpu.VMEM((256, 256), BF16), pltpu.VMEM((3, T, 256), F32),
    ]
    return pl.pallas_call(
        functools.partial(_mix_body, tiles_per_seq=nT),
        grid=(n_tiles + 1,),
        in_specs=in_specs,
        out_specs=fin_tile(D),
        out_shape=jax.ShapeDtypeStruct((B, S, D), F32),
        scratch_shapes=scratch,
        compiler_params=pltpu.CompilerParams(
            dimension_semantics=("arbitrary",), vmem_limit_bytes=VMEM_LIMIT),
        name="mix",
    )(acts, dec, sbs, sbg, sbr, h, p, wout, wpg, wpe, bpg, lnw, lnb, snw, gnw, rnw, rnb, dsk,
      lg_row, lg_vec)


def _permute_proj(w):
    pieces = [w[:, _Z0:_DT0], w[:, _GQ0:_GA0], w[:, _RQ0:_NIN], w[:, _DT0:_GQ0], w[:, _GA0:_RQ0],
              jnp.zeros((w.shape[0], N_PROJ - _NIN), w.dtype)]
    return jnp.concatenate(pieces, axis=1).astype(BF16)


def kernel(x, p, w_in, conv_w, conv_b, dt_bias, a_log, d_skip, ssd_norm_w, gla_w_a2, gla_b_a,
           gla_norm_w, ret_norm_w, ret_norm_b, w_out, ln_w, ln_b, w_pe, w_pg, b_pg):
    B, S, D = x.shape
    assert D == D_MODEL and S % TILE == 0 and TILE % GLA_CHUNK == 0
    half = RET_DK // 2
    inv = ROPE_BASE ** (-jnp.arange(half, dtype=F32) / half)
    cos_t, sin_t = _rope_table(jnp.tile(inv, 256 // half)[None, :], S)
    log_gamma = jnp.log(1.0 - 2.0 ** (-5.0 - jnp.arange(4, dtype=F32)))
    lg_row = jnp.repeat(log_gamma, 64)[None, :]
    pad112 = jnp.zeros((112,), F32)
    h = x
    for i in range(DEPTH):
        w_p = _permute_proj(w_in[i])
        dtb_row = jnp.concatenate([dt_bias[i].reshape(16), pad112])[None, :]
        alog_row = jnp.concatenate([a_log[i].reshape(16), pad112])[None, :]
        wsm = jnp.zeros((128, 256), F32)
        wsm = wsm.at[16:32, 0:128].set(gla_w_a2[i, 0]).at[32:48, 128:256].set(gla_w_a2[i, 1]).astype(BF16)
        acts, dec, sbs, sbg, sbr = _prep_call(
            h, w_p, conv_w[i], conv_b[i][None, :], dtb_row, alog_row, wsm,
            gla_b_a[i].reshape(1, 256), cos_t, sin_t, lg_row)
        h = _mix_call(
            acts, dec, sbs, sbg, sbr, h, p, i,
            w_out[i].astype(BF16), w_pg[i].astype(BF16), w_pe[i].astype(BF16), b_pg[i][None, :],
            ln_w[i][None, :], ln_b[i][None, :], ssd_norm_w[i][None, :],
            jnp.tile(gla_norm_w[i], 4)[None, :], ret_norm_w[i][None, :], ret_norm_b[i][None, :],
            jnp.repeat(d_skip[i], 64)[None, :], lg_row, log_gamma)
    return h
```

```python
import functools

import jax
import jax.numpy as jnp
from jax import lax
from jax.experimental import pallas as pl
from jax.experimental.pallas import tpu as pltpu

F32 = jnp.float32
BF16 = jnp.bfloat16

D_MODEL = 1024
D_PLE = 256
DEPTH = 2
SSD_HEADS = 8
SSD_WIDTH = 512
SSD_STATE = 128
SSD_CONV = 5
SSD_CONV_CH = 1024
GLA_DK = 32
GLA_TEMP = 16.0
RET_DK = 64
ROPE_BASE = 10000.0
DN_ALPHA = float((2 * DEPTH) ** 0.25)
LN_EPS = 1e-5
RMS_EPS = 1e-6
LOG2E = 1.4426950408889634

_Z0, _XBC0, _DT0, _GQ0, _GK0, _GV0, _GG0, _GA0, _RQ0, _RK0, _RV0, _RG0, _NIN = (
    0, 512, 1536, 1552, 1680, 1808, 2064, 2320, 2352, 2608, 2864, 3120, 3376)
N_PROJ = 3456
REST0 = 1536
N_ACT = 3328
N_DEC = 512

TILE = 256
PREP_SUB = 2
HALO = 16
GLA_CHUNK = 64
VMEM_LIMIT = 56 * 1024 * 1024

PREP_ORDER = (
    "x1", "x0", "small", "c0", "c1", "x2", "rq_d", "x3", "c2", "rv_d", "c3", "z1", "z0", "rk_d", "rk_v",
    "gqk_d", "rq_v", "sb_out1", "gv_d", "gg_d", "st_r1", "rg_d", "st_s1", "st_g1", "sb_out0", "st_r0",
    "st_s0", "st_g0")
MIX_ORDER = (
    "cg", "cum", "raw_r", "raw_s", "e1", "pe", "s_vec", "s_exp", "s_h3", "s_h0", "s_h1", "s_h7", "s_h4",
    "s_h6", "r_y0", "r_y1", "r_y3", "s_h5", "ln", "e3", "r_inter", "r_upd", "s_h2", "g_prep", "g_s3", "r_y2",
    "g_a0", "out", "g_a3", "g_s1", "g_a1", "g_s2", "g_a2", "g_s0", "s_st0", "s_st1", "s_fin", "g_rec",
    "g_i0", "g_i1", "g_i2", "g_i3", "g_norm", "r_norm")


def _mm(a, b):
    return jnp.dot(a, b, preferred_element_type=F32)


def _mm_nt(a, b):
    return lax.dot_general(a, b, (((1,), (1,)), ((), ())), preferred_element_type=F32)


def _mm_tn(a, b):
    return lax.dot_general(a, b, (((0,), (0,)), ((), ())), preferred_element_type=F32)


def _iota(shape, dim):
    return lax.broadcasted_iota(jnp.int32, shape, dim)


def _split(x, parts):
    out, r = [], x
    for k in range(parts):
        piece = r.astype(BF16)
        out.append(piece)
        if k + 1 < parts:
            r = r - piece.astype(F32)
    return jnp.concatenate(out, axis=1)


def _fold(y, parts):
    w = y.shape[1] // parts
    acc = y[:, :w]
    for k in range(1, parts):
        acc = acc + y[:, k * w:(k + 1) * w]
    return acc


def _cumsum_rows(tri, x):
    return _fold(_mm(tri, _split(x, 3)), 3)


def _head_expander(first_lane, parts):
    r = _iota((128 * parts, SSD_WIDTH), 0) & 127
    c = _iota((128 * parts, SSD_WIDTH), 1) >> 6
    return jnp.where(r == c + first_lane, 1.0, 0.0).astype(BF16)


def _expand(blk, expander):
    return _mm(_split(blk, 2), expander)


def _sigmoid(x):
    return 1.0 / (1.0 + jnp.exp(-x))


def _softplus(x):
    return jnp.maximum(x, 0.0) + jnp.log1p(jnp.exp(-jnp.abs(x)))


def _tile_rows(x, n):
    return jnp.concatenate([x] * n, axis=0)


def _run_stages(stages, order):
    assert sorted(order) == sorted(stages), "every stage exactly once"
    for name in order:
        stages[name]()


def _rope_body(inv_ref, cos_ref, sin_ref):
    rows = cos_ref.shape[0]
    pos = (_iota((rows, 256), 0) + pl.program_id(0) * rows).astype(F32)
    ang = pos * inv_ref[...]
    first_half = (_iota((rows, 256), 1) & 63) < 32
    s = jnp.sin(ang)
    cos_ref[...] = jnp.cos(ang)
    sin_ref[...] = jnp.where(first_half, -s, s)


def _rope_table(inv_row, seq):
    rows = 512
    return pl.pallas_call(
        _rope_body,
        grid=(seq // rows,),
        in_specs=[pl.BlockSpec((1, 256), lambda i: (0, 0))],
        out_specs=[pl.BlockSpec((rows, 256), lambda i: (i, 0))] * 2,
        out_shape=[jax.ShapeDtypeStruct((seq, 256), F32)] * 2,
        name="rope_table",
    )(inv_row)


def _prep_body(hp_ref, h_ref, hn_ref, w_ref, cw_ref, cb_ref, dtb_ref, alog_ref, wsm_ref, ba_ref,
               cos_ref, sin_ref, lg_ref,
               acts_ref, dec_ref, sbs_ref, sbg_ref, sbr_ref,
               st_s, st_g, st_r, tri_ref, expb_ref, rtab_ref):
    T = TILE
    P = PREP_SUB * TILE
    i = pl.program_id(1)
    n = pl.num_programs(1)
    t = n - 1 - i

    @pl.when(i == 0)
    def _():
        st_s[...] = jnp.zeros_like(st_s)
        st_g[...] = jnp.zeros_like(st_g)
        st_r[...] = jnp.zeros_like(st_r)

    @pl.when((pl.program_id(0) == 0) & (i == 0))
    def _():
        tri_ref[...] = jnp.where(_iota((T, T), 0) >= _iota((T, T), 1), 1.0, 0.0).astype(BF16)
        expb_ref[...] = _head_expander(SSD_HEADS, 2)
        rtab_ref[...] = jnp.exp(_iota((T, 256), 0).astype(F32) * lg_ref[...])

    hp = jnp.where(t > 0, hp_ref[...], 0.0)
    hn = jnp.where(t < n - 1, hn_ref[...], 0.0)
    hext = jnp.concatenate([hp, h_ref[...], hn], axis=0).astype(BF16)
    hm = hext[HALO:HALO + P]
    first_half = (_iota((P, 256), 1) & 63) < 32
    sub_rows = [slice(T * s, T * s + T) for s in range(PREP_SUB)]
    v = {}

    def proj(lhs, first, width=256):
        return _mm(lhs, w_ref[:, first:first + width])

    def rotary(u):
        swapped = jnp.where(first_half, pltpu.roll(u, 224, 1), pltpu.roll(u, 32, 1))
        return u * cos_ref[...] + swapped * sin_ref[...]

    def small():
        sm = proj(hm, REST0 + 1792, 128)
        lane128 = _iota((1, 128), 1)
        dt = _softplus(sm + dtb_ref[...])
        la_s = dt * jnp.where(lane128 < 2 * SSD_HEADS, -jnp.exp(alog_ref[...]), 0.0)
        pre = _mm(sm.astype(BF16), wsm_ref[...]) + ba_ref[...]
        la_g = -_softplus(-pre) * (1.0 / GLA_TEMP)
        dec_ref[:, 0:128] = dt
        dec_ref[:, 128:256] = la_s
        dec_ref[:, 256:512] = la_g
        lab = jnp.concatenate([la_s, la_g[:, 128:256]], axis=1)
        for s, rows in enumerate(sub_rows):
            cum = _cumsum_rows(tri_ref[...], lab[rows])
            v["dt", s] = dt[rows]
            v["cbx", s] = cum - lab[rows]
            v["tot", s] = cum[T - 1:T, :]

    def x_piece(c):
        v["xs", c] = proj(hext, _XBC0 + 256 * c)

    def conv_piece(c):
        cols = slice(256 * c, 256 * c + 256)
        xs = v.pop(("xs", c))
        acc = jnp.broadcast_to(cb_ref[:, cols], (P, 256))
        for j in range(SSD_CONV):
            r0 = HALO - SSD_CONV // 2 + j
            acc = acc + cw_ref[j:j + 1, cols] * xs[r0:r0 + P, :]
        v["xbc", c] = acc * _sigmoid(acc)
        acts_ref[:, cols] = v["xbc", c].astype(BF16)

    def z_piece(c):
        acts_ref[:, 1024 + 256 * c:1280 + 256 * c] = proj(hm, 256 * c).astype(BF16)

    def rq_d():
        v["rq"] = proj(hm, REST0 + 768)

    def rq_v():
        acts_ref[:, 2304:2560] = rotary(v.pop("rq")).astype(BF16)

    def rk_d():
        v["rk_raw"] = proj(hm, REST0 + 1024)

    def rk_v():
        v["rk"] = rotary(v.pop("rk_raw")) * (RET_DK ** -0.5)
        acts_ref[:, 2560:2816] = v["rk"].astype(BF16)

    def rv_d():
        v["rv"] = proj(hm, REST0 + 1280).astype(BF16)
        acts_ref[:, 2816:3072] = v["rv"]

    def gqk_d():
        gqk = proj(hm, REST0)
        v["gk"] = gqk[:, 128:256]
        acts_ref[:, 1536:1664] = (gqk[:, 0:128] * (GLA_DK ** -0.5)).astype(BF16)
        acts_ref[:, 1664:1792] = v["gk"].astype(BF16)

    def gv_d():
        v["gv"] = proj(hm, REST0 + 256).astype(BF16)
        acts_ref[:, 1792:2048] = v["gv"]

    def gg_d():
        acts_ref[:, 2048:2304] = proj(hm, REST0 + 512).astype(BF16)

    def rg_d():
        acts_ref[:, 3072:3328] = proj(hm, REST0 + 1536).astype(BF16)

    def sb_out(s):
        sbs_ref[s] = st_s[...]
        sbg_ref[s] = st_g[...]
        sbr_ref[s] = st_r[...]

    def st_s_update(s):
        tile = sub_rows[s]
        exp_b = expb_ref[...]
        wx = _expand(jnp.exp(v["cbx", s][:, 0:128]) * v["dt", s], exp_b)
        dec_row = _expand(jnp.broadcast_to(jnp.exp(v["tot", s][:, 0:128]), (8, 128)), exp_b)[0:1]
        bm = v["xbc", 2][tile].astype(BF16)
        for g in range(2):
            rows = slice(128 * g, 128 * g + 128)
            cols = slice(256 * g, 256 * g + 256)
            xw = (v["xbc", g][tile] * wx[:, cols]).astype(BF16)
            st_s[rows, :] = st_s[rows, :] * dec_row[:, cols] + _mm_tn(bm[:, rows], xw)

    def st_g_update(s):
        tile = sub_rows[s]
        mask_g = (_iota((256, 128), 0) >> 6) == (_iota((256, 128), 1) >> 5)
        kw = (v["gk"][tile] * jnp.exp(v["cbx", s][:, 128:256])).astype(BF16)
        st_g[...] = (st_g[...] * jnp.exp(v["tot", s][:, 128:256])
                     + jnp.where(mask_g, _mm_tn(v["gv"][tile], kw), 0.0))

    def st_r_update(s):
        tile = sub_rows[s]
        mask_r = (_iota((256, 256), 0) >> 6) == (_iota((256, 256), 1) >> 6)
        kwr = (v["rk"][tile] * rtab_ref[...]).astype(BF16)
        st_r[...] = (st_r[...] * jnp.exp(float(T) * lg_ref[...])
                     + jnp.where(mask_r, _mm_tn(v["rv"][tile], kwr), 0.0))

    stages = {"small": small, "rq_d": rq_d, "rq_v": rq_v, "rk_d": rk_d, "rk_v": rk_v, "rv_d": rv_d,
              "gqk_d": gqk_d, "gv_d": gv_d, "gg_d": gg_d, "rg_d": rg_d}
    for s in range(PREP_SUB):
        stages[f"sb_out{s}"] = functools.partial(sb_out, s)
        stages[f"st_s{s}"] = functools.partial(st_s_update, s)
        stages[f"st_g{s}"] = functools.partial(st_g_update, s)
        stages[f"st_r{s}"] = functools.partial(st_r_update, s)
    for c in range(4):
        stages[f"x{c}"] = functools.partial(x_piece, c)
        stages[f"c{c}"] = functools.partial(conv_piece, c)
    for c in range(2):
        stages[f"z{c}"] = functools.partial(z_piece, c)
    _run_stages(stages, PREP_ORDER)


def _prep_call(h, w_p, cw, cb, dtb_row, alog_row, wsm, ba_row, cos_t, sin_t, lg_row):
    B, S, D = h.shape
    T = TILE
    P = PREP_SUB * TILE
    nT = S // T
    nP = S // P
    hb = P // HALO
    const = lambda shape: pl.BlockSpec(shape, lambda b, i: (0,) * len(shape))
    in_specs = [
        pl.BlockSpec((None, HALO, D), lambda b, i: (b, jnp.maximum((nP - 1 - i) * hb - 1, 0), 0)),
        pl.BlockSpec((None, P, D), lambda b, i: (b, nP - 1 - i, 0)),
        pl.BlockSpec((None, HALO, D), lambda b, i: (b, jnp.minimum((nP - i) * hb, S // HALO - 1), 0)),
        const((D, N_PROJ)), const((SSD_CONV, SSD_CONV_CH)), const((1, SSD_CONV_CH)),
        const((1, 128)), const((1, 128)), const((128, 256)), const((1, 256)),
        pl.BlockSpec((P, 256), lambda b, i: (nP - 1 - i, 0)),
        pl.BlockSpec((P, 256), lambda b, i: (nP - 1 - i, 0)),
        const((1, 256)),
    ]
    out_specs = [
        pl.BlockSpec((None, P, N_ACT), lambda b, i: (b, nP - 1 - i, 0)),
        pl.BlockSpec((None, P, N_DEC), lambda b, i: (b, nP - 1 - i, 0)),
        pl.BlockSpec((None, PREP_SUB, 256, 256), lambda b, i: (b, nP - 1 - i, 0, 0)),
        pl.BlockSpec((None, PREP_SUB, 256, 128), lambda b, i: (b, nP - 1 - i, 0, 0)),
        pl.BlockSpec((None, PREP_SUB, 256, 256), lambda b, i: (b, nP - 1 - i, 0, 0)),
    ]
    out_shape = [
        jax.ShapeDtypeStruct((B, S, N_ACT), BF16),
        jax.ShapeDtypeStruct((B, S, N_DEC), F32),
        jax.ShapeDtypeStruct((B, nT, 256, 256), F32),
        jax.ShapeDtypeStruct((B, nT, 256, 128), F32),
        jax.ShapeDtypeStruct((B, nT, 256, 256), F32),
    ]
    scratch = [pltpu.VMEM((256, 256), F32), pltpu.VMEM((256, 128), F32), pltpu.VMEM((256, 256), F32),
               pltpu.VMEM((T, T), BF16), pltpu.VMEM((256, SSD_WIDTH), BF16), pltpu.VMEM((T, 256), F32)]
    return pl.pallas_call(
        _prep_body,
        grid=(B, nP),
        in_specs=in_specs,
        out_specs=out_specs,
        out_shape=out_shape,
        scratch_shapes=scratch,
        compiler_params=pltpu.CompilerParams(
            dimension_semantics=("arbitrary", "arbitrary"), vmem_limit_bytes=VMEM_LIMIT),
        name="prep",
    )(h, h, h, w_p, cw, cb, dtb_row, alog_row, wsm, ba_row, cos_t, sin_t, lg_row)


def _group_mean(x, ones_blocks):
    return _mm(x.astype(BF16), ones_blocks) * (1.0 / 64.0)


def _mix_body(acts_ref, dec_ref, sbs_ref, sbg_ref, sbr_ref, h_ref, p_ref,
              wout_ref, wpg_ref, wpe_ref, bpg_ref, lnw_ref, lnb_ref,
              snw_ref, gnw_ref, rnw_ref, rnb_ref, dsk_ref, lg_ref, lgs_ref,
              out_ref, sf_s, sf_g, sf_r, dm_ref, ycat_ref,
              tri_ref, tric_ref, expf_ref, expb_ref, ones_ref, rtab_ref, *, tiles_per_seq):
    T = TILE
    H = T // 2
    L = GLA_CHUNK
    nc = T // L
    step = pl.program_id(0)
    scan_tile = jnp.minimum(step, pl.num_programs(0) - 2)

    @pl.when(lax.rem(scan_tile, tiles_per_seq) == 0)
    def _():
        sf_s[...] = jnp.zeros_like(sf_s)
        sf_g[...] = jnp.zeros_like(sf_g)
        sf_r[...] = jnp.zeros_like(sf_r)

    row = _iota((T, T), 0)
    col = _iota((T, T), 1)
    lower = row >= col
    mask_v = (_iota((256, 256), 0) >> 6) == (_iota((256, 256), 1) >> 6)

    @pl.when(step == 0)
    def _():
        ycat_ref[...] = jnp.zeros_like(ycat_ref)
        dist = jnp.abs(row - col).astype(F32)
        for hh in range(4):
            dm_ref[hh] = jnp.exp(dist * lgs_ref[hh])
        tri_ref[...] = jnp.where(lower, 1.0, 0.0).astype(BF16)
        tric_ref[...] = jnp.where(lower & ((row >> 6) == (col >> 6)), 1.0, 0.0).astype(BF16)
        expf_ref[...] = _head_expander(0, 2)
        expb_ref[...] = _head_expander(SSD_HEADS, 2)
        ones_ref[...] = jnp.where(mask_v, 1.0, 0.0).astype(BF16)
        tpos = _iota((T, 256), 0).astype(F32)
        rtab_ref[0] = jnp.exp((tpos + 1.0) * lg_ref[...])
        rtab_ref[1] = jnp.exp((float(T) - tpos) * lg_ref[...])
        rtab_ref[2] = jnp.exp((float(T - 1) - tpos) * lg_ref[...])

    mask_k = (_iota((256, 128), 0) >> 6) == (_iota((256, 128), 1) >> 5)
    mask_kk = (_iota((256, 256), 0) >> 6) == ((_iota((256, 256), 1) & 127) >> 5)
    lane_head = _iota((T, 256), 1) >> 6
    lower_h = lower[0:H, 0:H]
    lower_c = _iota((L, 256), 0) >= (_iota((L, 256), 1) & (L - 1))
    zero_bf = jnp.zeros((T, 256), BF16)
    group_rows = [slice(128 * g, 128 * g + 128) for g in range(2)]
    group_cols = [slice(256 * g, 256 * g + 256) for g in range(2)]
    x_bf = acts_ref[:, 0:512]
    b_bf = acts_ref[:, 512:768]
    c_bf = acts_ref[:, 768:1024]
    q_bf = acts_ref[:, 2304:2560]
    k_bf = acts_ref[:, 2560:2816]
    v_r = acts_ref[:, 2816:3072]
    v_bf = acts_ref[:, 1792:2048]
    v = {"y_g": [jnp.zeros((T, 256), F32), jnp.zeros((T, 256), F32)], "o_r": jnp.zeros((T, 256), F32),
         "y_c": [None] * nc, "a_c": [None] * nc}

    def cum():
        v["cum"] = _cumsum_rows(tri_ref[...], dec_ref[:, 128:256])

    def raw_s():
        v["raw_s"] = [_mm_nt(c_bf[:, group_rows[g]], b_bf[:, group_rows[g]]) for g in range(2)]

    def s_vec():
        cm = v["cum"]
        cbx = cm - dec_ref[:, 128:256]
        ldt = jnp.log(dec_ref[:, 0:128])
        v["col_f"] = cm * LOG2E
        v["col_b"] = cbx * LOG2E
        v["row_f"] = ((cm - ldt) * LOG2E).T
        v["row_b"] = ((cbx + ldt) * LOG2E).T

    def s_exp():
        dt = dec_ref[:, 0:128]
        cm = v["cum"]
        cbx = cm - dec_ref[:, 128:256]
        tot = cm[T - 1:T, :]
        exp_f = expf_ref[...]
        v["e_f"] = _expand(jnp.exp(cm), exp_f)
        v["e_b"] = _expand(jnp.exp(tot - cbx), expb_ref[...])
        w_f = _expand(jnp.exp(tot - cm) * dt, exp_f)
        v["dec_f"] = _expand(jnp.broadcast_to(jnp.exp(tot), (8, 128)), exp_f)[0:1]
        v["x_f"] = x_bf.astype(F32)
        v["xw"] = (v["x_f"] * w_f).astype(BF16)

    def s_head(hd):
        g, hh = divmod(hd, 4)
        hb = SSD_HEADS + hd
        cf_h, cb_h = v["col_f"][:, hd:hd + 1], v["col_b"][:, hb:hb + 1]
        rf_h, rb_h = v["row_f"][hd:hd + 1, :], v["row_b"][hb:hb + 1, :]
        top = jnp.concatenate(
            [jnp.where(lower_h, cf_h[0:H] - rf_h[:, 0:H], rb_h[:, 0:H] - cb_h[0:H]),
             rb_h[:, H:T] - cb_h[0:H]], axis=1)
        bot = jnp.concatenate(
            [cf_h[H:T] - rf_h[:, 0:H],
             jnp.where(lower_h, cf_h[H:T] - rf_h[:, H:T], rb_h[:, H:T] - cb_h[H:T])], axis=1)
        m = jnp.exp2(jnp.concatenate([top, bot], axis=0))
        s = (v["raw_s"][g] * m).astype(BF16)
        x_g = x_bf[:, group_cols[g]]
        v["y_g"][g] = v["y_g"][g] + _mm(s, jnp.where(lane_head == hh, x_g, jnp.zeros_like(x_g)))

    def s_state(g):
        rows, cols = group_rows[g], group_cols[g]
        st = jnp.concatenate([sf_s[rows, :], sbs_ref[rows, :]], axis=1).astype(BF16)
        cs = _mm(c_bf[:, rows], st)
        v["y_g"][g] = v["y_g"][g] + v["e_f"][:, cols] * cs[:, 0:256] + v["e_b"][:, cols] * cs[:, 256:512]
        sf_s[rows, :] = sf_s[rows, :] * v["dec_f"][:, cols] + _mm_tn(b_bf[:, rows], v["xw"][:, cols])

    def s_fin():
        y_ssd = jnp.concatenate(v["y_g"], axis=1) + dsk_ref[...] * v["x_f"]
        z = acts_ref[:, 1024:1536].astype(F32)
        yz = y_ssd * (z * _sigmoid(z))
        y1 = yz * lax.rsqrt(jnp.mean(yz * yz, axis=-1, keepdims=True) + RMS_EPS) * snw_ref[...]
        ycat_ref[:, 0:512] = y1.astype(BF16)

    def raw_r():
        v["raw_r"] = [_mm_nt(jnp.where(lane_head == hh, q_bf, zero_bf), k_bf) for hh in range(4)]

    def r_y(hh):
        s = (v["raw_r"][hh] * dm_ref[hh]).astype(BF16)
        v["o_r"] = v["o_r"] + _mm(s, jnp.where(lane_head == hh, v_r, zero_bf))

    def r_inter():
        q_f32 = q_bf.astype(F32)
        qq = jnp.concatenate([q_f32 * rtab_ref[0], q_f32 * rtab_ref[1]], axis=1).astype(BF16)
        ss = jnp.concatenate([sf_r[...], sbr_ref[...]], axis=1).astype(BF16)
        v["o_r"] = v["o_r"] + _mm_nt(qq, ss)

    def r_upd():
        kwf = (k_bf.astype(F32) * rtab_ref[2]).astype(BF16)
        sf_r[...] = sf_r[...] * jnp.exp(float(T) * lg_ref[...]) + jnp.where(mask_v, _mm_tn(v_r, kwf), 0.0)

    def r_norm():
        o_r = v["o_r"]
        mu = _group_mean(o_r, ones_ref[...])
        d = o_r - mu
        var = _group_mean(d * d, ones_ref[...])
        rg = acts_ref[:, 3072:3328].astype(F32)
        y3 = (d * lax.rsqrt(var + LN_EPS) * rnw_ref[...] + rnb_ref[...]) * (rg * _sigmoid(rg))
        ycat_ref[:, 768:1024] = y3.astype(BF16)

    def cg():
        v["cg"] = _cumsum_rows(tric_ref[...], dec_ref[:, 256:512])

    def g_prep():
        cgv = v["cg"]
        cf = cgv[:, 0:128]
        cbx_g = cgv[:, 128:256] - dec_ref[:, 384:512]
        v["last"] = [cgv[L * c + L - 1:L * c + L, :] for c in range(nc)]
        last_full = jnp.concatenate([jnp.broadcast_to(r, (L, 256)) for r in v["last"]], axis=0)
        q = acts_ref[:, 1536:1664].astype(F32)
        k = acts_ref[:, 1664:1792].astype(F32)
        v["qf"] = (q * jnp.exp(cf)).astype(BF16)
        v["kf"] = (k * jnp.exp(-cf)).astype(BF16)
        v["qb"] = (q * jnp.exp(-cbx_g)).astype(BF16)
        v["kb"] = (k * jnp.exp(cbx_g)).astype(BF16)
        v["ksf"] = (k * jnp.exp(last_full[:, 0:128] - cf)).astype(BF16)
        v["qbi"] = (q * jnp.exp(last_full[:, 128:256] - cbx_g)).astype(BF16)

    def g_score(c):
        sl = slice(L * c, L * c + L)
        kf_t = jnp.where(mask_k, _tile_rows(v["kf"][sl], 4), jnp.zeros((256, 128), BF16))
        kb_t = jnp.where(mask_k, _tile_rows(v["kb"][sl], 4), jnp.zeros((256, 128), BF16))
        sc = jnp.where(lower_c, _mm_nt(v["qf"][sl], kf_t), _mm_nt(v["qb"][sl], kb_t)).astype(BF16)
        vd = jnp.where(mask_v, _tile_rows(v_bf[sl], 4), jnp.zeros((256, 256), BF16))
        v["y_c"][c] = _mm(sc, vd)

    def g_contrib(c):
        sl = slice(L * c, L * c + L)
        a = _mm_tn(v_bf[sl], jnp.concatenate([v["ksf"][sl], v["kb"][sl]], axis=1))
        v["a_c"][c] = jnp.where(mask_kk, a, 0.0)

    def g_rec():
        s_f, s_b = [None] * nc, [None] * nc
        s = sf_g[...]
        for c in range(nc):
            s_f[c] = s
            s = s * jnp.exp(v["last"][c][:, 0:128]) + v["a_c"][c][:, 0:128]
        sf_g[...] = s
        s = sbg_ref[...]
        for c in reversed(range(nc)):
            s_b[c] = s
            s = s * jnp.exp(v["last"][c][:, 128:256]) + v["a_c"][c][:, 128:256]
        v["ss_c"] = [jnp.concatenate([s_f[c], s_b[c]], axis=1).astype(BF16) for c in range(nc)]

    def g_inter(c):
        sl = slice(L * c, L * c + L)
        qq = jnp.concatenate([v["qf"][sl], v["qbi"][sl]], axis=1)
        v["y_c"][c] = v["y_c"][c] + _mm_nt(qq, v["ss_c"][c])

    def g_norm():
        o_g = jnp.concatenate(v["y_c"], axis=0)
        gg = acts_ref[:, 2048:2304].astype(F32)
        ms = _group_mean(o_g * o_g, ones_ref[...])
        y2 = o_g * lax.rsqrt(ms + RMS_EPS) * gnw_ref[...] * (gg * _sigmoid(gg))
        ycat_ref[:, 512:768] = y2.astype(BF16)

    def by_columns(lhs, w_ref):
        return jnp.concatenate([_mm(lhs, w_ref[:, c:c + 256]) for c in range(0, w_ref.shape[1], 256)], axis=1)

    def e1():
        v["mix"] = by_columns(ycat_ref[...], wout_ref)

    def pe():
        v["pe"] = by_columns(p_ref[...].astype(BF16), wpe_ref)

    def ln():
        r = DN_ALPHA * h_ref[...] + v.pop("mix")
        mu_r = jnp.mean(r, axis=-1, keepdims=True)
        dr = r - mu_r
        var_r = jnp.mean(dr * dr, axis=-1, keepdims=True)
        v["hn"] = dr * lax.rsqrt(var_r + LN_EPS) * lnw_ref[...] + lnb_ref[...]

    def e3():
        v["gate_pre"] = by_columns(v["hn"].astype(BF16), wpg_ref)

    def out():
        out_ref[...] = v["hn"] + _sigmoid(v["gate_pre"] + bpg_ref[...]) * v["pe"]

    stages = {"cum": cum, "cg": cg, "raw_s": raw_s, "raw_r": raw_r, "e1": e1, "pe": pe, "s_vec": s_vec,
              "s_exp": s_exp, "ln": ln, "e3": e3, "s_fin": s_fin, "r_inter": r_inter, "r_upd": r_upd,
              "out": out, "g_prep": g_prep, "g_rec": g_rec, "g_norm": g_norm, "r_norm": r_norm}
    for hd in range(SSD_HEADS):
        stages[f"s_h{hd}"] = functools.partial(s_head, hd)
    for g in range(2):
        stages[f"s_st{g}"] = functools.partial(s_state, g)
    for hh in range(4):
        stages[f"r_y{hh}"] = functools.partial(r_y, hh)
    for c in range(nc):
        stages[f"g_s{c}"] = functools.partial(g_score, c)
        stages[f"g_a{c}"] = functools.partial(g_contrib, c)
        stages[f"g_i{c}"] = functools.partial(g_inter, c)
    _run_stages(stages, MIX_ORDER)


def _mix_call(acts, dec, sbs, sbg, sbr, h, p, layer, wout, wpg, wpe, bpg, lnw, lnb,
              snw, gnw, rnw, rnb, dsk, lg_row, lg_vec):
    B, S, D = h.shape
    T = TILE
    nT = S // T
    n_tiles = B * nT
    scan_bt = lambda s: (lax.div(jnp.minimum(s, n_tiles - 1), nT), lax.rem(jnp.minimum(s, n_tiles - 1), nT))
    fin_bt = lambda s: (lax.div(jnp.maximum(s - 1, 0), nT), lax.rem(jnp.maximum(s - 1, 0), nT))
    const = lambda shape: pl.BlockSpec(shape, lambda s: (0,) * len(shape))
    scan_tile = lambda width: pl.BlockSpec((None, T, width), lambda s: (*scan_bt(s), 0))
    fin_tile = lambda width: pl.BlockSpec((None, T, width), lambda s: (*fin_bt(s), 0))
    state = lambda width: pl.BlockSpec((None, None, 256, width), lambda s: (*scan_bt(s), 0, 0))
    in_specs = [
        scan_tile(N_ACT), scan_tile(N_DEC), state(256), state(128), state(256), fin_tile(D),
        pl.BlockSpec((None, None, T, D_PLE), lambda s: (layer, *fin_bt(s), 0)),
        const((D, D)), const((D, D)), const((D_PLE, D)), const((1, D)), const((1, D)), const((1, D)),
        const((1, SSD_WIDTH)), const((1, 256)), const((1, 256)), const((1, 256)), const((1, SSD_WIDTH)),
        const((1, 256)),
        pl.BlockSpec(memory_space=pltpu.SMEM),
    ]
    scratch = [
        pltpu.VMEM((256, 256), F32), pltpu.VMEM((256, 128), F32), pltpu.VMEM((256, 256), F32),
        pltpu.VMEM((4, T, T), F32), pltpu.VMEM((T, D), BF16),
        pltpu.VMEM((T, T), BF16), pltpu.VMEM((T, T), BF16),
        pltpu.VMEM((256, SSD_WIDTH), BF16), pltpu.VMEM((256, SSD_WIDTH), BF16),
        pltpu.VMEM((256, 256), BF16), pltpu.VMEM((3, T, 256), F32),
    ]
    return pl.pallas_call(
        functools.partial(_mix_body, tiles_per_seq=nT),
        grid=(n_tiles + 1,),
        in_specs=in_specs,
        out_specs=fin_tile(D),
        out_shape=jax.ShapeDtypeStruct((B, S, D), F32),
        scratch_shapes=scratch,
        compiler_params=pltpu.CompilerParams(
            dimension_semantics=("arbitrary",), vmem_limit_bytes=VMEM_LIMIT),
        name="mix",
    )(acts, dec, sbs, sbg, sbr, h, p, wout, wpg, wpe, bpg, lnw, lnb, snw, gnw, rnw, rnb, dsk,
      lg_row, lg_vec)


def _permute_proj(w):
    pieces = [w[:, _Z0:_DT0], w[:, _GQ0:_GA0], w[:, _RQ0:_NIN], w[:, _DT0:_GQ0], w[:, _GA0:_RQ0],
              jnp.zeros((w.shape[0], N_PROJ - _NIN), w.dtype)]
    return jnp.concatenate(pieces, axis=1).astype(BF16)


def kernel(x, p, w_in, conv_w, conv_b, dt_bias, a_log, d_skip, ssd_norm_w, gla_w_a2, gla_b_a,
           gla_norm_w, ret_norm_w, ret_norm_b, w_out, ln_w, ln_b, w_pe, w_pg, b_pg):
    B, S, D = x.shape
    assert D == D_MODEL and S % (PREP_SUB * TILE) == 0 and TILE % GLA_CHUNK == 0
    half = RET_DK // 2
    inv = ROPE_BASE ** (-jnp.arange(half, dtype=F32) / half)
    cos_t, sin_t = _rope_table(jnp.tile(inv, 256 // half)[None, :], S)
    log_gamma = jnp.log(1.0 - 2.0 ** (-5.0 - jnp.arange(4, dtype=F32)))
    lg_row = jnp.repeat(log_gamma, 64)[None, :]
    pad112 = jnp.zeros((112,), F32)
    h = x
    for i in range(DEPTH):
        w_p = _permute_proj(w_in[i])
        dtb_row = jnp.concatenate([dt_bias[i].reshape(16), pad112])[None, :]
        alog_row = jnp.concatenate([a_log[i].reshape(16), pad112])[None, :]
        wsm = jnp.zeros((128, 256), F32)
        wsm = wsm.at[16:32, 0:128].set(gla_w_a2[i, 0]).at[32:48, 128:256].set(gla_w_a2[i, 1]).astype(BF16)
        acts, dec, sbs, sbg, sbr = _prep_call(
            h, w_p, conv_w[i], conv_b[i][None, :], dtb_row, alog_row, wsm,
            gla_b_a[i].reshape(1, 256), cos_t, sin_t, lg_row)
        h = _mix_call(
            acts, dec, sbs, sbg, sbr, h, p, i,
            w_out[i].astype(BF16), w_pg[i].astype(BF16), w_pe[i].astype(BF16), b_pg[i][None, :],
            ln_w[i][None, :], ln_b[i][None, :], ssd_norm_w[i][None, :],
            jnp.tile(gla_norm_w[i], 4)[None, :], ret_norm_w[i][None, :], ret_norm_b[i][None, :],
            jnp.repeat(d_skip[i], 64)[None, :], lg_row, log_gamma)
    return h
```

```python
import functools

import jax
import jax.numpy as jnp
from jax import lax
from jax.experimental import pallas as pl
from jax.experimental.pallas import tpu as pltpu

F32 = jnp.float32
BF16 = jnp.bfloat16

D_MODEL = 1024
D_PLE = 256
DEPTH = 2
SSD_HEADS = 8
SSD_WIDTH = 512
SSD_STATE = 128
SSD_CONV = 5
SSD_CONV_CH = 1024
GLA_DK = 32
GLA_TEMP = 16.0
RET_DK = 64
ROPE_BASE = 10000.0
DN_ALPHA = float((2 * DEPTH) ** 0.25)
LN_EPS = 1e-5
RMS_EPS = 1e-6
LOG2E = 1.4426950408889634

_Z0, _XBC0, _DT0, _GQ0, _GK0, _GV0, _GG0, _GA0, _RQ0, _RK0, _RV0, _RG0, _NIN = (
    0, 512, 1536, 1552, 1680, 1808, 2064, 2320, 2352, 2608, 2864, 3120, 3376)
N_PROJ = 3456
REST0 = 1536
N_ACT = 3328
N_DEC = 512

TILE = 256
PREP_SUB = 2
HALO = 16
GLA_CHUNK = 64
VMEM_LIMIT = 56 * 1024 * 1024

PREP_ORDER = (
    "x0", "small", "x3", "x1", "c0", "c1", "z1", "x2", "rq_d", "z0", "rv_d", "c3", "c2", "rk_d", "rk_v",
    "gqk_d", "rq_v", "gv_d", "gg_d", "sb_out1", "st_r1", "rg_d", "st_s1", "st_g1", "sb_out0", "st_s0",
    "st_g0", "st_r0")
MIX_ORDER = (
    "cg", "cum", "raw_r", "raw_s", "e1", "pe", "s_vec", "s_exp", "s_m3", "s_d3", "s_m0", "s_d0", "s_m1",
    "s_d1", "s_m7", "s_d7", "s_m4", "s_d4", "s_m6", "s_d6", "r_y0", "r_y1", "r_y3", "s_m5", "s_d5", "ln",
    "e3", "r_inter", "r_upd", "s_m2", "s_d2", "g_prep", "g_s3", "r_y2", "g_a0", "out", "g_a3", "g_s1", "g_a1",
    "g_s2", "g_a2", "g_s0", "s_st0", "s_st1", "s_fin", "g_rec", "g_i0", "g_i1", "g_i2", "g_i3", "g_norm",
    "r_norm")


def _mm(a, b):
    return jnp.dot(a, b, preferred_element_type=F32)


def _mm_nt(a, b):
    return lax.dot_general(a, b, (((1,), (1,)), ((), ())), preferred_element_type=F32)


def _mm_tn(a, b):
    return lax.dot_general(a, b, (((0,), (0,)), ((), ())), preferred_element_type=F32)


def _iota(shape, dim):
    return lax.broadcasted_iota(jnp.int32, shape, dim)


def _split(x, parts):
    out, r = [], x
    for k in range(parts):
        piece = r.astype(BF16)
        out.append(piece)
        if k + 1 < parts:
            r = r - piece.astype(F32)
    return jnp.concatenate(out, axis=1)


def _fold(y, parts):
    w = y.shape[1] // parts
    acc = y[:, :w]
    for k in range(1, parts):
        acc = acc + y[:, k * w:(k + 1) * w]
    return acc


def _cumsum_rows(tri, x, parts=3):
    return _fold(_mm(tri, _split(x, parts)), parts)


def _head_expander(first_lane, parts):
    r = _iota((128 * parts, SSD_WIDTH), 0) & 127
    c = _iota((128 * parts, SSD_WIDTH), 1) >> 6
    return jnp.where(r == c + first_lane, 1.0, 0.0).astype(BF16)


def _expand(blk, expander):
    return _mm(_split(blk, 2), expander)


def _sigmoid(x):
    return 1.0 / (1.0 + jnp.exp(-x))


def _softplus(x):
    return jnp.maximum(x, 0.0) + jnp.log1p(jnp.exp(-jnp.abs(x)))


def _tile_rows(x, n):
    return jnp.concatenate([x] * n, axis=0)


def _run_stages(stages, order):
    assert sorted(order) == sorted(stages), "every stage exactly once"
    for name in order:
        stages[name]()


def _rope_body(inv_ref, cos_ref, sin_ref):
    rows = cos_ref.shape[0]
    pos = (_iota((rows, 256), 0) + pl.program_id(0) * rows).astype(F32)
    ang = pos * inv_ref[...]
    first_half = (_iota((rows, 256), 1) & 63) < 32
    s = jnp.sin(ang)
    cos_ref[...] = jnp.cos(ang)
    sin_ref[...] = jnp.where(first_half, -s, s)


def _rope_table(inv_row, seq):
    rows = 512
    return pl.pallas_call(
        _rope_body,
        grid=(seq // rows,),
        in_specs=[pl.BlockSpec((1, 256), lambda i: (0, 0))],
        out_specs=[pl.BlockSpec((rows, 256), lambda i: (i, 0))] * 2,
        out_shape=[jax.ShapeDtypeStruct((seq, 256), F32)] * 2,
        name="rope_table",
    )(inv_row)


def _prep_body(hp_ref, h_ref, hn_ref, w_ref, cw_ref, cb_ref, dtb_ref, alog_ref, wsm_ref, ba_ref,
               cos_ref, sin_ref, lg_ref,
               acts_ref, dec_ref, sbs_ref, sbg_ref, sbr_ref,
               st_s, st_g, st_r, tri_ref, expb_ref, rtab_ref):
    T = TILE
    P = PREP_SUB * TILE
    i = pl.program_id(1)
    n = pl.num_programs(1)
    t = n - 1 - i

    @pl.when(i == 0)
    def _():
        st_s[...] = jnp.zeros_like(st_s)
        st_g[...] = jnp.zeros_like(st_g)
        st_r[...] = jnp.zeros_like(st_r)

    @pl.when((pl.program_id(0) == 0) & (i == 0))
    def _():
        tri_ref[...] = jnp.where(_iota((T, T), 0) >= _iota((T, T), 1), 1.0, 0.0).astype(BF16)
        expb_ref[...] = _head_expander(SSD_HEADS, 2)
        rtab_ref[...] = jnp.exp(_iota((T, 256), 0).astype(F32) * lg_ref[...])

    hp = jnp.where(t > 0, hp_ref[...], 0.0)
    hn = jnp.where(t < n - 1, hn_ref[...], 0.0)
    hext = jnp.concatenate([hp, h_ref[...], hn], axis=0).astype(BF16)
    hm = hext[HALO:HALO + P]
    first_half = (_iota((P, 256), 1) & 63) < 32
    sub_rows = [slice(T * s, T * s + T) for s in range(PREP_SUB)]
    v = {}

    def proj(lhs, first, width=256):
        return _mm(lhs, w_ref[:, first:first + width])

    def rotary(u):
        swapped = jnp.where(first_half, pltpu.roll(u, 224, 1), pltpu.roll(u, 32, 1))
        return u * cos_ref[...] + swapped * sin_ref[...]

    def small():
        sm = proj(hm, REST0 + 1792, 128)
        lane128 = _iota((1, 128), 1)
        dt = _softplus(sm + dtb_ref[...])
        la_s = dt * jnp.where(lane128 < 2 * SSD_HEADS, -jnp.exp(alog_ref[...]), 0.0)
        pre = _mm(sm.astype(BF16), wsm_ref[...]) + ba_ref[...]
        la_g = -_softplus(-pre) * (1.0 / GLA_TEMP)
        dec_ref[:, 0:128] = dt
        dec_ref[:, 128:256] = la_s
        dec_ref[:, 256:512] = la_g
        lab = jnp.concatenate([la_s, la_g[:, 128:256]], axis=1)
        for s, rows in enumerate(sub_rows):
            cum = _cumsum_rows(tri_ref[...], lab[rows])
            v["dt", s] = dt[rows]
            v["cbx", s] = cum - lab[rows]
            v["tot", s] = cum[T - 1:T, :]

    def x_piece(c):
        v["xs", c] = proj(hext, _XBC0 + 256 * c)

    def conv_piece(c):
        cols = slice(256 * c, 256 * c + 256)
        xs = v.pop(("xs", c))
        acc = jnp.broadcast_to(cb_ref[:, cols], (P, 256))
        for j in range(SSD_CONV):
            r0 = HALO - SSD_CONV // 2 + j
            acc = acc + cw_ref[j:j + 1, cols] * xs[r0:r0 + P, :]
        v["xbc", c] = acc * _sigmoid(acc)
        acts_ref[:, cols] = v["xbc", c].astype(BF16)

    def silu(u):
        return u * _sigmoid(u)

    def z_piece(c):
        acts_ref[:, 1024 + 256 * c:1280 + 256 * c] = silu(proj(hm, 256 * c)).astype(BF16)

    def rq_d():
        v["rq"] = proj(hm, REST0 + 768)

    def rq_v():
        acts_ref[:, 2304:2560] = rotary(v.pop("rq")).astype(BF16)

    def rk_d():
        v["rk_raw"] = proj(hm, REST0 + 1024)

    def rk_v():
        v["rk"] = rotary(v.pop("rk_raw")) * (RET_DK ** -0.5)
        acts_ref[:, 2560:2816] = v["rk"].astype(BF16)

    def rv_d():
        v["rv"] = proj(hm, REST0 + 1280).astype(BF16)
        acts_ref[:, 2816:3072] = v["rv"]

    def gqk_d():
        gqk = proj(hm, REST0)
        v["gk"] = gqk[:, 128:256]
        acts_ref[:, 1536:1664] = (gqk[:, 0:128] * (GLA_DK ** -0.5)).astype(BF16)
        acts_ref[:, 1664:1792] = v["gk"].astype(BF16)

    def gv_d():
        v["gv"] = proj(hm, REST0 + 256).astype(BF16)
        acts_ref[:, 1792:2048] = v["gv"]

    def gg_d():
        acts_ref[:, 2048:2304] = silu(proj(hm, REST0 + 512)).astype(BF16)

    def rg_d():
        acts_ref[:, 3072:3328] = silu(proj(hm, REST0 + 1536)).astype(BF16)

    def sb_out(s):
        sbs_ref[s] = st_s[...]
        sbg_ref[s] = st_g[...]
        sbr_ref[s] = st_r[...]

    def st_s_update(s):
        tile = sub_rows[s]
        exp_b = expb_ref[...]
        wx = _expand(jnp.exp(v["cbx", s][:, 0:128]) * v["dt", s], exp_b)
        dec_row = _expand(jnp.broadcast_to(jnp.exp(v["tot", s][:, 0:128]), (8, 128)), exp_b)[0:1]
        bm = v["xbc", 2][tile].astype(BF16)
        for g in range(2):
            rows = slice(128 * g, 128 * g + 128)
            cols = slice(256 * g, 256 * g + 256)
            xw = (v["xbc", g][tile] * wx[:, cols]).astype(BF16)
            st_s[rows, :] = st_s[rows, :] * dec_row[:, cols] + _mm_tn(bm[:, rows], xw)

    def st_g_update(s):
        tile = sub_rows[s]
        mask_g = (_iota((256, 128), 0) >> 6) == (_iota((256, 128), 1) >> 5)
        kw = (v["gk"][tile] * jnp.exp(v["cbx", s][:, 128:256])).astype(BF16)
        st_g[...] = (st_g[...] * jnp.exp(v["tot", s][:, 128:256])
                     + jnp.where(mask_g, _mm_tn(v["gv"][tile], kw), 0.0))

    def st_r_update(s):
        tile = sub_rows[s]
        mask_r = (_iota((256, 256), 0) >> 6) == (_iota((256, 256), 1) >> 6)
        kwr = (v["rk"][tile] * rtab_ref[...]).astype(BF16)
        st_r[...] = (st_r[...] * jnp.exp(float(T) * lg_ref[...])
                     + jnp.where(mask_r, _mm_tn(v["rv"][tile], kwr), 0.0))

    stages = {"small": small, "rq_d": rq_d, "rq_v": rq_v, "rk_d": rk_d, "rk_v": rk_v, "rv_d": rv_d,
              "gqk_d": gqk_d, "gv_d": gv_d, "gg_d": gg_d, "rg_d": rg_d}
    for s in range(PREP_SUB):
        stages[f"sb_out{s}"] = functools.partial(sb_out, s)
        stages[f"st_s{s}"] = functools.partial(st_s_update, s)
        stages[f"st_g{s}"] = functools.partial(st_g_update, s)
        stages[f"st_r{s}"] = functools.partial(st_r_update, s)
    for c in range(4):
        stages[f"x{c}"] = functools.partial(x_piece, c)
        stages[f"c{c}"] = functools.partial(conv_piece, c)
    for c in range(2):
        stages[f"z{c}"] = functools.partial(z_piece, c)
    _run_stages(stages, PREP_ORDER)


def _prep_call(h, w_p, cw, cb, dtb_row, alog_row, wsm, ba_row, cos_t, sin_t, lg_row):
    B, S, D = h.shape
    T = TILE
    P = PREP_SUB * TILE
    nT = S // T
    nP = S // P
    hb = P // HALO
    const = lambda shape: pl.BlockSpec(shape, lambda b, i: (0,) * len(shape))
    in_specs = [
        pl.BlockSpec((None, HALO, D), lambda b, i: (b, jnp.maximum((nP - 1 - i) * hb - 1, 0), 0)),
        pl.BlockSpec((None, P, D), lambda b, i: (b, nP - 1 - i, 0)),
        pl.BlockSpec((None, HALO, D), lambda b, i: (b, jnp.minimum((nP - i) * hb, S // HALO - 1), 0)),
        const((D, N_PROJ)), const((SSD_CONV, SSD_CONV_CH)), const((1, SSD_CONV_CH)),
        const((1, 128)), const((1, 128)), const((128, 256)), const((1, 256)),
        pl.BlockSpec((P, 256), lambda b, i: (nP - 1 - i, 0)),
        pl.BlockSpec((P, 256), lambda b, i: (nP - 1 - i, 0)),
        const((1, 256)),
    ]
    out_specs = [
        pl.BlockSpec((None, P, N_ACT), lambda b, i: (b, nP - 1 - i, 0)),
        pl.BlockSpec((None, P, N_DEC), lambda b, i: (b, nP - 1 - i, 0)),
        pl.BlockSpec((None, PREP_SUB, 256, 256), lambda b, i: (b, nP - 1 - i, 0, 0)),
        pl.BlockSpec((None, PREP_SUB, 256, 128), lambda b, i: (b, nP - 1 - i, 0, 0)),
        pl.BlockSpec((None, PREP_SUB, 256, 256), lambda b, i: (b, nP - 1 - i, 0, 0)),
    ]
    out_shape = [
        jax.ShapeDtypeStruct((B, S, N_ACT), BF16),
        jax.ShapeDtypeStruct((B, S, N_DEC), F32),
        jax.ShapeDtypeStruct((B, nT, 256, 256), F32),
        jax.ShapeDtypeStruct((B, nT, 256, 128), F32),
        jax.ShapeDtypeStruct((B, nT, 256, 256), F32),
    ]
    scratch = [pltpu.VMEM((256, 256), F32), pltpu.VMEM((256, 128), F32), pltpu.VMEM((256, 256), F32),
               pltpu.VMEM((T, T), BF16), pltpu.VMEM((256, SSD_WIDTH), BF16), pltpu.VMEM((T, 256), F32)]
    return pl.pallas_call(
        _prep_body,
        grid=(B, nP),
        in_specs=in_specs,
        out_specs=out_specs,
        out_shape=out_shape,
        scratch_shapes=scratch,
        compiler_params=pltpu.CompilerParams(
            dimension_semantics=("arbitrary", "arbitrary"), vmem_limit_bytes=VMEM_LIMIT),
        name="prep",
    )(h, h, h, w_p, cw, cb, dtb_row, alog_row, wsm, ba_row, cos_t, sin_t, lg_row)


def _group_mean(x, ones_blocks):
    return _mm(x.astype(BF16), ones_blocks) * (1.0 / 64.0)


def _mix_body(acts_ref, dec_ref, sbs_ref, sbg_ref, sbr_ref, h_ref, p_ref,
              wout_ref, wpg_ref, wpe_ref, bpg_ref, lnw_ref, lnb_ref,
              snw_ref, gnw_ref, rnw_ref, rnb_ref, dsk_ref, lg_ref, lgs_ref,
              out_ref, sf_s, sf_g, sf_r, dm_ref, ycat_ref,
              tri_ref, tric_ref, expf_ref, expb_ref, ones_ref, rtab_ref, *, tiles_per_seq):
    T = TILE
    H = T // 2
    L = GLA_CHUNK
    nc = T // L
    step = pl.program_id(0)
    scan_tile = jnp.minimum(step, pl.num_programs(0) - 2)

    @pl.when(lax.rem(scan_tile, tiles_per_seq) == 0)
    def _():
        sf_s[...] = jnp.zeros_like(sf_s)
        sf_g[...] = jnp.zeros_like(sf_g)
        sf_r[...] = jnp.zeros_like(sf_r)

    row = _iota((T, T), 0)
    col = _iota((T, T), 1)
    lower = row >= col
    mask_v = (_iota((256, 256), 0) >> 6) == (_iota((256, 256), 1) >> 6)

    @pl.when(step == 0)
    def _():
        ycat_ref[...] = jnp.zeros_like(ycat_ref)
        dist = jnp.abs(row - col).astype(F32)
        for hh in range(4):
            dm_ref[hh] = jnp.exp(dist * lgs_ref[hh])
        tri_ref[...] = jnp.where(lower, 1.0, 0.0).astype(BF16)
        tric_ref[...] = jnp.where(lower & ((row >> 6) == (col >> 6)), 1.0, 0.0).astype(BF16)
        expf_ref[...] = _head_expander(0, 2)
        expb_ref[...] = _head_expander(SSD_HEADS, 2)
        ones_ref[...] = jnp.where(mask_v, 1.0, 0.0).astype(BF16)
        tpos = _iota((T, 256), 0).astype(F32)
        rtab_ref[0] = jnp.exp((tpos + 1.0) * lg_ref[...])
        rtab_ref[1] = jnp.exp((float(T) - tpos) * lg_ref[...])
        rtab_ref[2] = jnp.exp((float(T - 1) - tpos) * lg_ref[...])

    mask_k = (_iota((256, 128), 0) >> 6) == (_iota((256, 128), 1) >> 5)
    mask_kk = (_iota((256, 256), 0) >> 6) == ((_iota((256, 256), 1) & 127) >> 5)
    lane_head = _iota((T, 256), 1) >> 6
    lower_h = lower[0:H, 0:H]
    lower_c = _iota((L, 256), 0) >= (_iota((L, 256), 1) & (L - 1))
    zero_bf = jnp.zeros((T, 256), BF16)
    group_rows = [slice(128 * g, 128 * g + 128) for g in range(2)]
    group_cols = [slice(256 * g, 256 * g + 256) for g in range(2)]
    x_bf = acts_ref[:, 0:512]
    b_bf = acts_ref[:, 512:768]
    c_bf = acts_ref[:, 768:1024]
    q_bf = acts_ref[:, 2304:2560]
    k_bf = acts_ref[:, 2560:2816]
    v_r = acts_ref[:, 2816:3072]
    v_bf = acts_ref[:, 1792:2048]
    v = {"y_g": [jnp.zeros((T, 256), F32), jnp.zeros((T, 256), F32)], "o_r": jnp.zeros((T, 256), F32),
         "y_c": [None] * nc, "a_c": [None] * nc}

    def cum():
        v["cum"] = _cumsum_rows(tri_ref[...], dec_ref[:, 128:256])

    def raw_s():
        v["raw_s"] = [_mm_nt(c_bf[:, group_rows[g]], b_bf[:, group_rows[g]]) for g in range(2)]

    def s_vec():
        cm = v["cum"]
        cbx = cm - dec_ref[:, 128:256]
        ldt = jnp.log(dec_ref[:, 0:128])
        v["col_f"] = cm * LOG2E
        v["col_b"] = cbx * LOG2E
        v["row_f"] = ((cm - ldt) * LOG2E).T
        v["row_b"] = ((cbx + ldt) * LOG2E).T

    def s_exp():
        dt = dec_ref[:, 0:128]
        cm = v["cum"]
        cbx = cm - dec_ref[:, 128:256]
        tot = cm[T - 1:T, :]
        exp_f = expf_ref[...]
        v["e_f"] = _expand(jnp.exp(cm), exp_f)
        v["e_b"] = _expand(jnp.exp(tot - cbx), expb_ref[...])
        w_f = _expand(jnp.exp(tot - cm) * dt, exp_f)
        v["dec_f"] = _expand(jnp.broadcast_to(jnp.exp(tot), (8, 128)), exp_f)[0:1]
        v["x_f"] = x_bf.astype(F32)
        v["xw"] = (v["x_f"] * w_f).astype(BF16)

    def s_head(hd):
        g, hh = divmod(hd, 4)
        hb = SSD_HEADS + hd
        cf_h, cb_h = v["col_f"][:, hd:hd + 1], v["col_b"][:, hb:hb + 1]
        rf_h, rb_h = v["row_f"][hd:hd + 1, :], v["row_b"][hb:hb + 1, :]
        top = jnp.concatenate(
            [jnp.where(lower_h, cf_h[0:H] - rf_h[:, 0:H], rb_h[:, 0:H] - cb_h[0:H]),
             rb_h[:, H:T] - cb_h[0:H]], axis=1)
        bot = jnp.concatenate(
            [cf_h[H:T] - rf_h[:, 0:H],
             jnp.where(lower_h, cf_h[H:T] - rf_h[:, H:T], rb_h[:, H:T] - cb_h[H:T])], axis=1)
        m = jnp.exp2(jnp.concatenate([top, bot], axis=0))
        v["s", hd] = (v["raw_s"][g] * m).astype(BF16)

    def s_dot(hd):
        g, hh = divmod(hd, 4)
        x_g = x_bf[:, group_cols[g]]
        v["y_g"][g] = v["y_g"][g] + _mm(v.pop(("s", hd)), jnp.where(lane_head == hh, x_g, jnp.zeros_like(x_g)))

    def s_state(g):
        rows, cols = group_rows[g], group_cols[g]
        st = jnp.concatenate([sf_s[rows, :], sbs_ref[rows, :]], axis=1).astype(BF16)
        cs = _mm(c_bf[:, rows], st)
        v["y_g"][g] = v["y_g"][g] + v["e_f"][:, cols] * cs[:, 0:256] + v["e_b"][:, cols] * cs[:, 256:512]
        sf_s[rows, :] = sf_s[rows, :] * v["dec_f"][:, cols] + _mm_tn(b_bf[:, rows], v["xw"][:, cols])

    def s_fin():
        y_ssd = jnp.concatenate(v["y_g"], axis=1) + dsk_ref[...] * v["x_f"]
        yz = y_ssd * acts_ref[:, 1024:1536].astype(F32)
        y1 = yz * lax.rsqrt(jnp.mean(yz * yz, axis=-1, keepdims=True) + RMS_EPS) * snw_ref[...]
        ycat_ref[:, 0:512] = y1.astype(BF16)

    def raw_r():
        v["raw_r"] = [_mm_nt(jnp.where(lane_head == hh, q_bf, zero_bf), k_bf) for hh in range(4)]

    def r_y(hh):
        s = (v["raw_r"][hh] * dm_ref[hh]).astype(BF16)
        v["o_r"] = v["o_r"] + _mm(s, jnp.where(lane_head == hh, v_r, zero_bf))

    def r_inter():
        q_f32 = q_bf.astype(F32)
        qq = jnp.concatenate([q_f32 * rtab_ref[0], q_f32 * rtab_ref[1]], axis=1).astype(BF16)
        ss = jnp.concatenate([sf_r[...], sbr_ref[...]], axis=1).astype(BF16)
        v["o_r"] = v["o_r"] + _mm_nt(qq, ss)

    def r_upd():
        kwf = (k_bf.astype(F32) * rtab_ref[2]).astype(BF16)
        sf_r[...] = sf_r[...] * jnp.exp(float(T) * lg_ref[...]) + jnp.where(mask_v, _mm_tn(v_r, kwf), 0.0)

    def r_norm():
        o_r = v["o_r"]
        mu = _group_mean(o_r, ones_ref[...])
        d = o_r - mu
        var = _group_mean(d * d, ones_ref[...])
        gate = acts_ref[:, 3072:3328].astype(F32)
        y3 = (d * lax.rsqrt(var + LN_EPS) * rnw_ref[...] + rnb_ref[...]) * gate
        ycat_ref[:, 768:1024] = y3.astype(BF16)

    def cg():
        v["cg"] = _cumsum_rows(tric_ref[...], dec_ref[:, 256:512], parts=2)

    def g_prep():
        cgv = v["cg"]
        cf = cgv[:, 0:128]
        cbx_g = cgv[:, 128:256] - dec_ref[:, 384:512]
        v["last"] = [cgv[L * c + L - 1:L * c + L, :] for c in range(nc)]
        last_full = jnp.concatenate([jnp.broadcast_to(r, (L, 256)) for r in v["last"]], axis=0)
        q = acts_ref[:, 1536:1664].astype(F32)
        k = acts_ref[:, 1664:1792].astype(F32)
        v["qf"] = (q * jnp.exp(cf)).astype(BF16)
        v["kf"] = (k * jnp.exp(-cf)).astype(BF16)
        v["qb"] = (q * jnp.exp(-cbx_g)).astype(BF16)
        v["kb"] = (k * jnp.exp(cbx_g)).astype(BF16)
        v["ksf"] = (k * jnp.exp(last_full[:, 0:128] - cf)).astype(BF16)
        v["qbi"] = (q * jnp.exp(last_full[:, 128:256] - cbx_g)).astype(BF16)

    def g_score(c):
        sl = slice(L * c, L * c + L)
        kf_t = jnp.where(mask_k, _tile_rows(v["kf"][sl], 4), jnp.zeros((256, 128), BF16))
        kb_t = jnp.where(mask_k, _tile_rows(v["kb"][sl], 4), jnp.zeros((256, 128), BF16))
        sc = jnp.where(lower_c, _mm_nt(v["qf"][sl], kf_t), _mm_nt(v["qb"][sl], kb_t)).astype(BF16)
        vd = jnp.where(mask_v, _tile_rows(v_bf[sl], 4), jnp.zeros((256, 256), BF16))
        v["y_c"][c] = _mm(sc, vd)

    def g_contrib(c):
        sl = slice(L * c, L * c + L)
        a = _mm_tn(v_bf[sl], jnp.concatenate([v["ksf"][sl], v["kb"][sl]], axis=1))
        v["a_c"][c] = jnp.where(mask_kk, a, 0.0)

    def g_rec():
        s_f, s_b = [None] * nc, [None] * nc
        s = sf_g[...]
        for c in range(nc):
            s_f[c] = s
            s = s * jnp.exp(v["last"][c][:, 0:128]) + v["a_c"][c][:, 0:128]
        sf_g[...] = s
        s = sbg_ref[...]
        for c in reversed(range(nc)):
            s_b[c] = s
            s = s * jnp.exp(v["last"][c][:, 128:256]) + v["a_c"][c][:, 128:256]
        v["ss_c"] = [jnp.concatenate([s_f[c], s_b[c]], axis=1).astype(BF16) for c in range(nc)]

    def g_inter(c):
        sl = slice(L * c, L * c + L)
        qq = jnp.concatenate([v["qf"][sl], v["qbi"][sl]], axis=1)
        v["y_c"][c] = v["y_c"][c] + _mm_nt(qq, v["ss_c"][c])

    def g_norm():
        o_g = jnp.concatenate(v["y_c"], axis=0)
        gg = acts_ref[:, 2048:2304].astype(F32)
        ms = _group_mean(o_g * o_g, ones_ref[...])
        y2 = o_g * lax.rsqrt(ms + RMS_EPS) * gnw_ref[...] * gg
        ycat_ref[:, 512:768] = y2.astype(BF16)

    def by_columns(lhs, w_ref):
        return jnp.concatenate([_mm(lhs, w_ref[:, c:c + 256]) for c in range(0, w_ref.shape[1], 256)], axis=1)

    def e1():
        v["mix"] = by_columns(ycat_ref[...], wout_ref)

    def pe():
        v["pe"] = by_columns(p_ref[...].astype(BF16), wpe_ref)

    def ln():
        r = DN_ALPHA * h_ref[...] + v.pop("mix")
        mu_r = jnp.mean(r, axis=-1, keepdims=True)
        dr = r - mu_r
        var_r = jnp.mean(dr * dr, axis=-1, keepdims=True)
        v["hn"] = dr * lax.rsqrt(var_r + LN_EPS) * lnw_ref[...] + lnb_ref[...]

    def e3():
        v["gate_pre"] = by_columns(v["hn"].astype(BF16), wpg_ref)

    def out():
        out_ref[...] = v["hn"] + _sigmoid(v["gate_pre"] + bpg_ref[...]) * v["pe"]

    stages = {"cum": cum, "cg": cg, "raw_s": raw_s, "raw_r": raw_r, "e1": e1, "pe": pe, "s_vec": s_vec,
              "s_exp": s_exp, "ln": ln, "e3": e3, "s_fin": s_fin, "r_inter": r_inter, "r_upd": r_upd,
              "out": out, "g_prep": g_prep, "g_rec": g_rec, "g_norm": g_norm, "r_norm": r_norm}
    for hd in range(SSD_HEADS):
        stages[f"s_m{hd}"] = functools.partial(s_head, hd)
        stages[f"s_d{hd}"] = functools.partial(s_dot, hd)
    for g in range(2):
        stages[f"s_st{g}"] = functools.partial(s_state, g)
    for hh in range(4):
        stages[f"r_y{hh}"] = functools.partial(r_y, hh)
    for c in range(nc):
        stages[f"g_s{c}"] = functools.partial(g_score, c)
        stages[f"g_a{c}"] = functools.partial(g_contrib, c)
        stages[f"g_i{c}"] = functools.partial(g_inter, c)
    _run_stages(stages, MIX_ORDER)


def _mix_call(acts, dec, sbs, sbg, sbr, h, p, layer, wout, wpg, wpe, bpg, lnw, lnb,
              snw, gnw, rnw, rnb, dsk, lg_row, lg_vec):
    B, S, D = h.shape
    T = TILE
    nT = S // T
    n_tiles = B * nT
    scan_bt = lambda s: (lax.div(jnp.minimum(s, n_tiles - 1), nT), lax.rem(jnp.minimum(s, n_tiles - 1), nT))
    fin_bt = lambda s: (lax.div(jnp.maximum(s - 1, 0), nT), lax.rem(jnp.maximum(s - 1, 0), nT))
    const = lambda shape: pl.BlockSpec(shape, lambda s: (0,) * len(shape))
    scan_tile = lambda width: pl.BlockSpec((None, T, width), lambda s: (*scan_bt(s), 0))
    fin_tile = lambda width: pl.BlockSpec((None, T, width), lambda s: (*fin_bt(s), 0))
    state = lambda width: pl.BlockSpec((None, None, 256, width), lambda s: (*scan_bt(s), 0, 0))
    in_specs = [
        scan_tile(N_ACT), scan_tile(N_DEC), state(256), state(128), state(256), fin_tile(D),
        pl.BlockSpec((None, None, T, D_PLE), lambda s: (layer, *fin_bt(s), 0)),
        const((D, D)), const((D, D)), const((D_PLE, D)), const((1, D)), const((1, D)), const((1, D)),
        const((1, SSD_WIDTH)), const((1, 256)), const((1, 256)), const((1, 256)), const((1, SSD_WIDTH)),
        const((1, 256)),
        pl.BlockSpec(memory_space=pltpu.SMEM),
    ]
    scratch = [
        pltpu.VMEM((256, 256), F32), pltpu.VMEM((256, 128), F32), pltpu.VMEM((256, 256), F32),
        pltpu.VMEM((4, T, T), F32), pltpu.VMEM((T, D), BF16),
        pltpu.VMEM((T, T), BF16), pltpu.VMEM((T, T), BF16),
        pltpu.VMEM((256, SSD_WIDTH), BF16), pltpu.VMEM((256, SSD_WIDTH), BF16),
        pltpu.VMEM((256, 256), BF16), pltpu.VMEM((3, T, 256), F32),
    ]
    return pl.pallas_call(
        functools.partial(_mix_body, tiles_per_seq=nT),
        grid=(n_tiles + 1,),
        in_specs=in_specs,
        out_specs=fin_tile(D),
        out_shape=jax.ShapeDtypeStruct((B, S, D), F32),
        scratch_shapes=scratch,
        compiler_params=pltpu.CompilerParams(
            dimension_semantics=("arbitrary",), vmem_limit_bytes=VMEM_LIMIT),
        name="mix",
    )(acts, dec, sbs, sbg, sbr, h, p, wout, wpg, wpe, bpg, lnw, lnb, snw, gnw, rnw, rnb, dsk,
      lg_row, lg_vec)


def _permute_proj(w):
    pieces = [w[:, _Z0:_DT0], w[:, _GQ0:_GA0], w[:, _RQ0:_NIN], w[:, _DT0:_GQ0], w[:, _GA0:_RQ0],
              jnp.zeros((w.shape[0], N_PROJ - _NIN), w.dtype)]
    return jnp.concatenate(pieces, axis=1).astype(BF16)


def kernel(x, p, w_in, conv_w, conv_b, dt_bias, a_log, d_skip, ssd_norm_w, gla_w_a2, gla_b_a,
           gla_norm_w, ret_norm_w, ret_norm_b, w_out, ln_w, ln_b, w_pe, w_pg, b_pg):
    B, S, D = x.shape
    assert D == D_MODEL and S % (PREP_SUB * TILE) == 0 and TILE % GLA_CHUNK == 0
    half = RET_DK // 2
    inv = ROPE_BASE ** (-jnp.arange(half, dtype=F32) / half)
    cos_t, sin_t = _rope_table(jnp.tile(inv, 256 // half)[None, :], S)
    log_gamma = jnp.log(1.0 - 2.0 ** (-5.0 - jnp.arange(4, dtype=F32)))
    lg_row = jnp.repeat(log_gamma, 64)[None, :]
    pad112 = jnp.zeros((112,), F32)
    h = x
    for i in range(DEPTH):
        w_p = _permute_proj(w_in[i])
        dtb_row = jnp.concatenate([dt_bias[i].reshape(16), pad112])[None, :]
        alog_row = jnp.concatenate([a_log[i].reshape(16), pad112])[None, :]
        wsm = jnp.zeros((128, 256), F32)
        wsm = wsm.at[16:32, 0:128].set(gla_w_a2[i, 0]).at[32:48, 128:256].set(gla_w_a2[i, 1]).astype(BF16)
        acts, dec, sbs, sbg, sbr = _prep_call(
            h, w_p, conv_w[i], conv_b[i][None, :], dtb_row, alog_row, wsm,
            gla_b_a[i].reshape(1, 256), cos_t, sin_t, lg_row)
        h = _mix_call(
            acts, dec, sbs, sbg, sbr, h, p, i,
            w_out[i].astype(BF16), w_pg[i].astype(BF16), w_pe[i].astype(BF16), b_pg[i][None, :],
            ln_w[i][None, :], ln_b[i][None, :], ssd_norm_w[i][None, :],
            jnp.tile(gla_norm_w[i], 4)[None, :], ret_norm_w[i][None, :], ret_norm_b[i][None, :],
            jnp.repeat(d_skip[i], 64)[None, :], lg_row, log_gamma)
    return h
```

```python
import functools

import jax
import jax.numpy as jnp
from jax import lax
from jax.experimental import pallas as pl
from jax.experimental.pallas import tpu as pltpu

F32 = jnp.float32
BF16 = jnp.bfloat16

D_MODEL = 1024
D_PLE = 256
DEPTH = 2
SSD_HEADS = 8
SSD_WIDTH = 512
SSD_STATE = 128
SSD_CONV = 5
SSD_CONV_CH = 1024
GLA_DK = 32
GLA_TEMP = 16.0
RET_DK = 64
ROPE_BASE = 10000.0
DN_ALPHA = float((2 * DEPTH) ** 0.25)
LN_EPS = 1e-5
RMS_EPS = 1e-6
LOG2E = 1.4426950408889634

_Z0, _XBC0, _DT0, _GQ0, _GK0, _GV0, _GG0, _GA0, _RQ0, _RK0, _RV0, _RG0, _NIN = (
    0, 512, 1536, 1552, 1680, 1808, 2064, 2320, 2352, 2608, 2864, 3120, 3376)
N_PROJ = 3456
REST0 = 1536
N_ACT = 3328
N_DEC = 512

TILE = 256
PREP_SUB = 4
HALO = 16
GLA_CHUNK = 64
VMEM_LIMIT = 56 * 1024 * 1024

PREP_ORDER = (
    "x0", "small", "x3", "x1", "c0", "c1", "x2", "rk_d", "rq_d", "z0", "c3", "c2", "z1", "rq_v", "rv_d",
    "rk_v", "gqk_d", "gv_d", "sb_out3", "st_r3", "rg_d", "st_s3", "gg_d", "st_g3", "sb_out2", "st_s2",
    "st_g2", "st_r2", "sb_out1", "st_s1", "st_g1", "st_r1", "sb_out0", "st_s0", "st_g0", "st_r0")
MIX_ORDER = (
    "cum", "cg", "raw_r", "raw_s", "e1", "pe", "s_vec", "s_exp", "s_m3", "s_m0", "s_d0", "s_m1", "s_d3",
    "s_m7", "s_d7", "s_m4", "s_d4", "s_m6", "r_y3", "s_d6", "r_y0", "r_y1", "s_d1", "s_m5", "s_d5", "ln",
    "e3", "r_inter", "r_upd", "s_m2", "s_d2", "g_prep", "g_s3", "r_y2", "g_a0", "out", "g_a3", "g_s1", "g_a1",
    "g_s2", "g_s0", "g_a2", "s_st0", "s_st1", "s_fin", "g_rec", "g_i0", "g_i2", "g_i1", "g_i3", "r_norm",
    "g_norm")


def _mm(a, b):
    return jnp.dot(a, b, preferred_element_type=F32)


def _mm_nt(a, b):
    return lax.dot_general(a, b, (((1,), (1,)), ((), ())), preferred_element_type=F32)


def _mm_tn(a, b):
    return lax.dot_general(a, b, (((0,), (0,)), ((), ())), preferred_element_type=F32)


def _iota(shape, dim):
    return lax.broadcasted_iota(jnp.int32, shape, dim)


def _split(x, parts):
    out, r = [], x
    for k in range(parts):
        piece = r.astype(BF16)
        out.append(piece)
        if k + 1 < parts:
            r = r - piece.astype(F32)
    return jnp.concatenate(out, axis=1)


def _fold(y, parts):
    w = y.shape[1] // parts
    acc = y[:, :w]
    for k in range(1, parts):
        acc = acc + y[:, k * w:(k + 1) * w]
    return acc


def _cumsum_rows(tri, x, parts=3):
    return _fold(_mm(tri, _split(x, parts)), parts)


def _head_expander(first_lane, parts):
    r = _iota((128 * parts, SSD_WIDTH), 0) & 127
    c = _iota((128 * parts, SSD_WIDTH), 1) >> 6
    return jnp.where(r == c + first_lane, 1.0, 0.0).astype(BF16)


def _expand(blk, expander):
    return _mm(_split(blk, 2), expander)


def _sigmoid(x):
    return 1.0 / (1.0 + jnp.exp(-x))


def _softplus(x):
    return jnp.maximum(x, 0.0) + jnp.log1p(jnp.exp(-jnp.abs(x)))


def _tile_rows(x, n):
    return jnp.concatenate([x] * n, axis=0)


def _run_stages(stages, order):
    assert sorted(order) == sorted(stages), "every stage exactly once"
    for name in order:
        stages[name]()


def _rope_body(inv_ref, cos_ref, sin_ref):
    rows = cos_ref.shape[0]
    pos = (_iota((rows, 256), 0) + pl.program_id(0) * rows).astype(F32)
    ang = pos * inv_ref[...]
    first_half = (_iota((rows, 256), 1) & 63) < 32
    s = jnp.sin(ang)
    cos_ref[...] = jnp.cos(ang)
    sin_ref[...] = jnp.where(first_half, -s, s)


def _rope_table(inv_row, seq):
    rows = 512
    return pl.pallas_call(
        _rope_body,
        grid=(seq // rows,),
        in_specs=[pl.BlockSpec((1, 256), lambda i: (0, 0))],
        out_specs=[pl.BlockSpec((rows, 256), lambda i: (i, 0))] * 2,
        out_shape=[jax.ShapeDtypeStruct((seq, 256), F32)] * 2,
        name="rope_table",
    )(inv_row)


def _prep_body(hp_ref, h_ref, hn_ref, w_ref, cw_ref, cb_ref, dtb_ref, alog_ref, wsm_ref, ba_ref,
               cos_ref, sin_ref, lg_ref,
               acts_ref, dec_ref, sbs_ref, sbg_ref, sbr_ref,
               st_s, st_g, st_r, tri_ref, expb_ref, rtab_ref):
    T = TILE
    P = PREP_SUB * TILE
    i = pl.program_id(1)
    n = pl.num_programs(1)
    t = n - 1 - i

    @pl.when(i == 0)
    def _():
        st_s[...] = jnp.zeros_like(st_s)
        st_g[...] = jnp.zeros_like(st_g)
        st_r[...] = jnp.zeros_like(st_r)

    @pl.when((pl.program_id(0) == 0) & (i == 0))
    def _():
        tri_ref[...] = jnp.where(_iota((T, T), 0) >= _iota((T, T), 1), 1.0, 0.0).astype(BF16)
        expb_ref[...] = _head_expander(SSD_HEADS, 2)
        rtab_ref[...] = jnp.exp(_iota((T, 256), 0).astype(F32) * lg_ref[...])

    hp = jnp.where(t > 0, hp_ref[...], 0.0)
    hn = jnp.where(t < n - 1, hn_ref[...], 0.0)
    hext = jnp.concatenate([hp, h_ref[...], hn], axis=0).astype(BF16)
    hm = hext[HALO:HALO + P]
    first_half = (_iota((P, 256), 1) & 63) < 32
    sub_rows = [slice(T * s, T * s + T) for s in range(PREP_SUB)]
    v = {}

    def proj(lhs, first, width=256):
        return _mm(lhs, w_ref[:, first:first + width])

    def rotary(u):
        swapped = jnp.where(first_half, pltpu.roll(u, 224, 1), pltpu.roll(u, 32, 1))
        return u * cos_ref[...] + swapped * sin_ref[...]

    def small():
        sm = proj(hm, REST0 + 1792, 128)
        lane128 = _iota((1, 128), 1)
        dt = _softplus(sm + dtb_ref[...])
        la_s = dt * jnp.where(lane128 < 2 * SSD_HEADS, -jnp.exp(alog_ref[...]), 0.0)
        pre = _mm(sm.astype(BF16), wsm_ref[...]) + ba_ref[...]
        la_g = -_softplus(-pre) * (1.0 / GLA_TEMP)
        dec_ref[:, 0:128] = dt
        dec_ref[:, 128:256] = la_s
        dec_ref[:, 256:512] = la_g
        lab = jnp.concatenate([la_s, la_g[:, 128:256]], axis=1)
        for s, rows in enumerate(sub_rows):
            cum = _cumsum_rows(tri_ref[...], lab[rows])
            v["dt", s] = dt[rows]
            v["cbx", s] = cum - lab[rows]
            v["tot", s] = cum[T - 1:T, :]

    def x_piece(c):
        v["xs", c] = proj(hext, _XBC0 + 256 * c)

    def conv_piece(c):
        cols = slice(256 * c, 256 * c + 256)
        xs = v.pop(("xs", c))
        acc = jnp.broadcast_to(cb_ref[:, cols], (P, 256))
        for j in range(SSD_CONV):
            r0 = HALO - SSD_CONV // 2 + j
            acc = acc + cw_ref[j:j + 1, cols] * xs[r0:r0 + P, :]
        v["xbc", c] = acc * _sigmoid(acc)
        acts_ref[:, cols] = v["xbc", c].astype(BF16)

    def silu(u):
        return u * _sigmoid(u)

    def z_piece(c):
        acts_ref[:, 1024 + 256 * c:1280 + 256 * c] = silu(proj(hm, 256 * c)).astype(BF16)

    def rq_d():
        v["rq"] = proj(hm, REST0 + 768)

    def rq_v():
        acts_ref[:, 2304:2560] = rotary(v.pop("rq")).astype(BF16)

    def rk_d():
        v["rk_raw"] = proj(hm, REST0 + 1024)

    def rk_v():
        v["rk"] = rotary(v.pop("rk_raw")) * (RET_DK ** -0.5)
        acts_ref[:, 2560:2816] = v["rk"].astype(BF16)

    def rv_d():
        v["rv"] = proj(hm, REST0 + 1280).astype(BF16)
        acts_ref[:, 2816:3072] = v["rv"]

    def gqk_d():
        gqk = proj(hm, REST0)
        v["gk"] = gqk[:, 128:256]
        acts_ref[:, 1536:1664] = (gqk[:, 0:128] * (GLA_DK ** -0.5)).astype(BF16)
        acts_ref[:, 1664:1792] = v["gk"].astype(BF16)

    def gv_d():
        v["gv"] = proj(hm, REST0 + 256).astype(BF16)
        acts_ref[:, 1792:2048] = v["gv"]

    def gg_d():
        acts_ref[:, 2048:2304] = silu(proj(hm, REST0 + 512)).astype(BF16)

    def rg_d():
        acts_ref[:, 3072:3328] = silu(proj(hm, REST0 + 1536)).astype(BF16)

    def sb_out(s):
        sbs_ref[s] = st_s[...]
        sbg_ref[s] = st_g[...]
        sbr_ref[s] = st_r[...]

    def st_s_update(s):
        tile = sub_rows[s]
        exp_b = expb_ref[...]
        wx = _expand(jnp.exp(v["cbx", s][:, 0:128]) * v["dt", s], exp_b)
        dec_row = _expand(jnp.broadcast_to(jnp.exp(v["tot", s][:, 0:128]), (8, 128)), exp_b)[0:1]
        bm = v["xbc", 2][tile].astype(BF16)
        for g in range(2):
            rows = slice(128 * g, 128 * g + 128)
            cols = slice(256 * g, 256 * g + 256)
            xw = (v["xbc", g][tile] * wx[:, cols]).astype(BF16)
            st_s[rows, :] = st_s[rows, :] * dec_row[:, cols] + _mm_tn(bm[:, rows], xw)

    def st_g_update(s):
        tile = sub_rows[s]
        mask_g = (_iota((256, 128), 0) >> 6) == (_iota((256, 128), 1) >> 5)
        kw = (v["gk"][tile] * jnp.exp(v["cbx", s][:, 128:256])).astype(BF16)
        st_g[...] = (st_g[...] * jnp.exp(v["tot", s][:, 128:256])
                     + jnp.where(mask_g, _mm_tn(v["gv"][tile], kw), 0.0))

    def st_r_update(s):
        tile = sub_rows[s]
        mask_r = (_iota((256, 256), 0) >> 6) == (_iota((256, 256), 1) >> 6)
        kwr = (v["rk"][tile] * rtab_ref[...]).astype(BF16)
        st_r[...] = (st_r[...] * jnp.exp(float(T) * lg_ref[...])
                     + jnp.where(mask_r, _mm_tn(v["rv"][tile], kwr), 0.0))

    stages = {"small": small, "rq_d": rq_d, "rq_v": rq_v, "rk_d": rk_d, "rk_v": rk_v, "rv_d": rv_d,
              "gqk_d": gqk_d, "gv_d": gv_d, "gg_d": gg_d, "rg_d": rg_d}
    for s in range(PREP_SUB):
        stages[f"sb_out{s}"] = functools.partial(sb_out, s)
        stages[f"st_s{s}"] = functools.partial(st_s_update, s)
        stages[f"st_g{s}"] = functools.partial(st_g_update, s)
        stages[f"st_r{s}"] = functools.partial(st_r_update, s)
    for c in range(4):
        stages[f"x{c}"] = functools.partial(x_piece, c)
        stages[f"c{c}"] = functools.partial(conv_piece, c)
    for c in range(2):
        stages[f"z{c}"] = functools.partial(z_piece, c)
    _run_stages(stages, PREP_ORDER)


def _prep_call(h, w_p, cw, cb, dtb_row, alog_row, wsm, ba_row, cos_t, sin_t, lg_row):
    B, S, D = h.shape
    T = TILE
    P = PREP_SUB * TILE
    nT = S // T
    nP = S // P
    hb = P // HALO
    const = lambda shape: pl.BlockSpec(shape, lambda b, i: (0,) * len(shape))
    in_specs = [
        pl.BlockSpec((None, HALO, D), lambda b, i: (b, jnp.maximum((nP - 1 - i) * hb - 1, 0), 0)),
        pl.BlockSpec((None, P, D), lambda b, i: (b, nP - 1 - i, 0)),
        pl.BlockSpec((None, HALO, D), lambda b, i: (b, jnp.minimum((nP - i) * hb, S // HALO - 1), 0)),
        const((D, N_PROJ)), const((SSD_CONV, SSD_CONV_CH)), const((1, SSD_CONV_CH)),
        const((1, 128)), const((1, 128)), const((128, 256)), const((1, 256)),
        pl.BlockSpec((P, 256), lambda b, i: (nP - 1 - i, 0)),
        pl.BlockSpec((P, 256), lambda b, i: (nP - 1 - i, 0)),
        const((1, 256)),
    ]
    out_specs = [
        pl.BlockSpec((None, P, N_ACT), lambda b, i: (b, nP - 1 - i, 0)),
        pl.BlockSpec((None, P, N_DEC), lambda b, i: (b, nP - 1 - i, 0)),
        pl.BlockSpec((None, PREP_SUB, 256, 256), lambda b, i: (b, nP - 1 - i, 0, 0)),
        pl.BlockSpec((None, PREP_SUB, 256, 128), lambda b, i: (b, nP - 1 - i, 0, 0)),
        pl.BlockSpec((None, PREP_SUB, 256, 256), lambda b, i: (b, nP - 1 - i, 0, 0)),
    ]
    out_shape = [
        jax.ShapeDtypeStruct((B, S, N_ACT), BF16),
        jax.ShapeDtypeStruct((B, S, N_DEC), F32),
        jax.ShapeDtypeStruct((B, nT, 256, 256), F32),
        jax.ShapeDtypeStruct((B, nT, 256, 128), F32),
        jax.ShapeDtypeStruct((B, nT, 256, 256), F32),
    ]
    scratch = [pltpu.VMEM((256, 256), F32), pltpu.VMEM((256, 128), F32), pltpu.VMEM((256, 256), F32),
               pltpu.VMEM((T, T), BF16), pltpu.VMEM((256, SSD_WIDTH), BF16), pltpu.VMEM((T, 256), F32)]
    return pl.pallas_call(
        _prep_body,
        grid=(B, nP),
        in_specs=in_specs,
        out_specs=out_specs,
        out_shape=out_shape,
        scratch_shapes=scratch,
        compiler_params=pltpu.CompilerParams(
            dimension_semantics=("arbitrary", "arbitrary"), vmem_limit_bytes=VMEM_LIMIT),
        name="prep",
    )(h, h, h, w_p, cw, cb, dtb_row, alog_row, wsm, ba_row, cos_t, sin_t, lg_row)


def _group_mean(x, ones_blocks):
    return _mm(x.astype(BF16), ones_blocks) * (1.0 / 64.0)


def _mix_body(acts_ref, dec_ref, sbs_ref, sbg_ref, sbr_ref, h_ref, p_ref,
              wout_ref, wpg_ref, wpe_ref, bpg_ref, lnw_ref, lnb_ref,
              snw_ref, gnw_ref, rnw_ref, rnb_ref, dsk_ref, lg_ref, lgs_ref,
              out_ref, sf_s, sf_g, sf_r, dm_ref, ycat_ref,
              tri_ref, tric_ref, expf_ref, expb_ref, ones_ref, rtab_ref, *, tiles_per_seq):
    T = TILE
    H = T // 2
    L = GLA_CHUNK
    nc = T // L
    step = pl.program_id(0)
    scan_tile = jnp.minimum(step, pl.num_programs(0) - 2)

    @pl.when(lax.rem(scan_tile, tiles_per_seq) == 0)
    def _():
        sf_s[...] = jnp.zeros_like(sf_s)
        sf_g[...] = jnp.zeros_like(sf_g)
        sf_r[...] = jnp.zeros_like(sf_r)

    row = _iota((T, T), 0)
    col = _iota((T, T), 1)
    lower = row >= col
    mask_v = (_iota((256, 256), 0) >> 6) == (_iota((256, 256), 1) >> 6)

    @pl.when(step == 0)
    def _():
        ycat_ref[...] = jnp.zeros_like(ycat_ref)
        dist = jnp.abs(row - col).astype(F32)
        for hh in range(4):
            dm_ref[hh] = jnp.exp(dist * lgs_ref[hh])
        tri_ref[...] = jnp.where(lower, 1.0, 0.0).astype(BF16)
        tric_ref[...] = jnp.where(lower & ((row >> 6) == (col >> 6)), 1.0, 0.0).astype(BF16)
        expf_ref[...] = _head_expander(0, 2)
        expb_ref[...] = _head_expander(SSD_HEADS, 2)
        ones_ref[...] = jnp.where(mask_v, 1.0, 0.0).astype(BF16)
        tpos = _iota((T, 256), 0).astype(F32)
        rtab_ref[0] = jnp.exp((tpos + 1.0) * lg_ref[...])
        rtab_ref[1] = jnp.exp((float(T) - tpos) * lg_ref[...])
        rtab_ref[2] = jnp.exp((float(T - 1) - tpos) * lg_ref[...])

    mask_k = (_iota((256, 128), 0) >> 6) == (_iota((256, 128), 1) >> 5)
    mask_kk = (_iota((256, 256), 0) >> 6) == ((_iota((256, 256), 1) & 127) >> 5)
    lane_head = _iota((T, 256), 1) >> 6
    lower_h = lower[0:H, 0:H]
    lower_c = _iota((L, 256), 0) >= (_iota((L, 256), 1) & (L - 1))
    zero_bf = jnp.zeros((T, 256), BF16)
    group_rows = [slice(128 * g, 128 * g + 128) for g in range(2)]
    group_cols = [slice(256 * g, 256 * g + 256) for g in range(2)]
    x_bf = acts_ref[:, 0:512]
    b_bf = acts_ref[:, 512:768]
    c_bf = acts_ref[:, 768:1024]
    q_bf = acts_ref[:, 2304:2560]
    k_bf = acts_ref[:, 2560:2816]
    v_r = acts_ref[:, 2816:3072]
    v_bf = acts_ref[:, 1792:2048]
    v = {"y_g": [jnp.zeros((T, 256), F32), jnp.zeros((T, 256), F32)], "o_r": jnp.zeros((T, 256), F32),
         "y_c": [None] * nc, "a_c": [None] * nc}

    def cum():
        v["cum"] = _cumsum_rows(tri_ref[...], dec_ref[:, 128:256])

    def raw_s():
        v["raw_s"] = [_mm_nt(c_bf[:, group_rows[g]], b_bf[:, group_rows[g]]) for g in range(2)]

    def s_vec():
        cm = v["cum"]
        cbx = cm - dec_ref[:, 128:256]
        ldt = jnp.log(dec_ref[:, 0:128])
        v["col_f"] = cm * LOG2E
        v["col_b"] = cbx * LOG2E
        v["row_f"] = ((cm - ldt) * LOG2E).T
        v["row_b"] = ((cbx + ldt) * LOG2E).T

    def s_exp():
        dt = dec_ref[:, 0:128]
        cm = v["cum"]
        cbx = cm - dec_ref[:, 128:256]
        tot = cm[T - 1:T, :]
        exp_f = expf_ref[...]
        v["e_f"] = _expand(jnp.exp(cm), exp_f)
        v["e_b"] = _expand(jnp.exp(tot - cbx), expb_ref[...])
        w_f = _expand(jnp.exp(tot - cm) * dt, exp_f)
        v["dec_f"] = _expand(jnp.broadcast_to(jnp.exp(tot), (8, 128)), exp_f)[0:1]
        v["x_f"] = x_bf.astype(F32)
        v["xw"] = (v["x_f"] * w_f).astype(BF16)

    def s_head(hd):
        g, hh = divmod(hd, 4)
        hb = SSD_HEADS + hd
        cf_h, cb_h = v["col_f"][:, hd:hd + 1], v["col_b"][:, hb:hb + 1]
        rf_h, rb_h = v["row_f"][hd:hd + 1, :], v["row_b"][hb:hb + 1, :]
        top = jnp.concatenate(
            [jnp.where(lower_h, cf_h[0:H] - rf_h[:, 0:H], rb_h[:, 0:H] - cb_h[0:H]),
             rb_h[:, H:T] - cb_h[0:H]], axis=1)
        bot = jnp.concatenate(
            [cf_h[H:T] - rf_h[:, 0:H],
             jnp.where(lower_h, cf_h[H:T] - rf_h[:, H:T], rb_h[:, H:T] - cb_h[H:T])], axis=1)
        m = jnp.exp2(jnp.concatenate([top, bot], axis=0))
        v["s", hd] = (v["raw_s"][g] * m).astype(BF16)

    def s_dot(hd):
        g, hh = divmod(hd, 4)
        x_g = x_bf[:, group_cols[g]]
        v["y_g"][g] = v["y_g"][g] + _mm(v.pop(("s", hd)), jnp.where(lane_head == hh, x_g, jnp.zeros_like(x_g)))

    def s_state(g):
        rows, cols = group_rows[g], group_cols[g]
        st = jnp.concatenate([sf_s[rows, :], sbs_ref[rows, :]], axis=1).astype(BF16)
        cs = _mm(c_bf[:, rows], st)
        v["y_g"][g] = v["y_g"][g] + v["e_f"][:, cols] * cs[:, 0:256] + v["e_b"][:, cols] * cs[:, 256:512]
        sf_s[rows, :] = sf_s[rows, :] * v["dec_f"][:, cols] + _mm_tn(b_bf[:, rows], v["xw"][:, cols])

    def s_fin():
        y_ssd = jnp.concatenate(v["y_g"], axis=1) + dsk_ref[...] * v["x_f"]
        yz = y_ssd * acts_ref[:, 1024:1536].astype(F32)
        y1 = yz * lax.rsqrt(jnp.mean(yz * yz, axis=-1, keepdims=True) + RMS_EPS) * snw_ref[...]
        ycat_ref[:, 0:512] = y1.astype(BF16)

    def raw_r():
        v["raw_r"] = [_mm_nt(jnp.where(lane_head == hh, q_bf, zero_bf), k_bf) for hh in range(4)]

    def r_y(hh):
        s = (v["raw_r"][hh] * dm_ref[hh]).astype(BF16)
        v["o_r"] = v["o_r"] + _mm(s, jnp.where(lane_head == hh, v_r, zero_bf))

    def r_inter():
        q_f32 = q_bf.astype(F32)
        qq = jnp.concatenate([q_f32 * rtab_ref[0], q_f32 * rtab_ref[1]], axis=1).astype(BF16)
        ss = jnp.concatenate([sf_r[...], sbr_ref[...]], axis=1).astype(BF16)
        v["o_r"] = v["o_r"] + _mm_nt(qq, ss)

    def r_upd():
        kwf = (k_bf.astype(F32) * rtab_ref[2]).astype(BF16)
        sf_r[...] = sf_r[...] * jnp.exp(float(T) * lg_ref[...]) + jnp.where(mask_v, _mm_tn(v_r, kwf), 0.0)

    def r_norm():
        o_r = v["o_r"]
        mu = _group_mean(o_r, ones_ref[...])
        d = o_r - mu
        var = _group_mean(d * d, ones_ref[...])
        gate = acts_ref[:, 3072:3328].astype(F32)
        y3 = (d * lax.rsqrt(var + LN_EPS) * rnw_ref[...] + rnb_ref[...]) * gate
        ycat_ref[:, 768:1024] = y3.astype(BF16)

    def cg():
        v["cg"] = _cumsum_rows(tric_ref[...], dec_ref[:, 256:512], parts=2)

    def g_prep():
        cgv = v["cg"]
        cf = cgv[:, 0:128]
        cbx_g = cgv[:, 128:256] - dec_ref[:, 384:512]
        v["last"] = [cgv[L * c + L - 1:L * c + L, :] for c in range(nc)]
        last_full = jnp.concatenate([jnp.broadcast_to(r, (L, 256)) for r in v["last"]], axis=0)
        q = acts_ref[:, 1536:1664].astype(F32)
        k = acts_ref[:, 1664:1792].astype(F32)
        v["qf"] = (q * jnp.exp(cf)).astype(BF16)
        v["kf"] = (k * jnp.exp(-cf)).astype(BF16)
        v["qb"] = (q * jnp.exp(-cbx_g)).astype(BF16)
        v["kb"] = (k * jnp.exp(cbx_g)).astype(BF16)
        v["ksf"] = (k * jnp.exp(last_full[:, 0:128] - cf)).astype(BF16)
        v["qbi"] = (q * jnp.exp(last_full[:, 128:256] - cbx_g)).astype(BF16)

    def g_score(c):
        sl = slice(L * c, L * c + L)
        kf_t = jnp.where(mask_k, _tile_rows(v["kf"][sl], 4), jnp.zeros((256, 128), BF16))
        kb_t = jnp.where(mask_k, _tile_rows(v["kb"][sl], 4), jnp.zeros((256, 128), BF16))
        sc = jnp.where(lower_c, _mm_nt(v["qf"][sl], kf_t), _mm_nt(v["qb"][sl], kb_t)).astype(BF16)
        vd = jnp.where(mask_v, _tile_rows(v_bf[sl], 4), jnp.zeros((256, 256), BF16))
        v["y_c"][c] = _mm(sc, vd)

    def g_contrib(c):
        sl = slice(L * c, L * c + L)
        a = _mm_tn(v_bf[sl], jnp.concatenate([v["ksf"][sl], v["kb"][sl]], axis=1))
        v["a_c"][c] = jnp.where(mask_kk, a, 0.0)

    def g_rec():
        s_f, s_b = [None] * nc, [None] * nc
        s = sf_g[...]
        for c in range(nc):
            s_f[c] = s
            s = s * jnp.exp(v["last"][c][:, 0:128]) + v["a_c"][c][:, 0:128]
        sf_g[...] = s
        s = sbg_ref[...]
        for c in reversed(range(nc)):
            s_b[c] = s
            s = s * jnp.exp(v["last"][c][:, 128:256]) + v["a_c"][c][:, 128:256]
        v["ss_c"] = [jnp.concatenate([s_f[c], s_b[c]], axis=1).astype(BF16) for c in range(nc)]

    def g_inter(c):
        sl = slice(L * c, L * c + L)
        qq = jnp.concatenate([v["qf"][sl], v["qbi"][sl]], axis=1)
        v["y_c"][c] = v["y_c"][c] + _mm_nt(qq, v["ss_c"][c])

    def g_norm():
        o_g = jnp.concatenate(v["y_c"], axis=0)
        gg = acts_ref[:, 2048:2304].astype(F32)
        ms = _group_mean(o_g * o_g, ones_ref[...])
        y2 = o_g * lax.rsqrt(ms + RMS_EPS) * gnw_ref[...] * gg
        ycat_ref[:, 512:768] = y2.astype(BF16)

    def by_columns(lhs, w_ref):
        return jnp.concatenate([_mm(lhs, w_ref[:, c:c + 256]) for c in range(0, w_ref.shape[1], 256)], axis=1)

    def e1():
        v["mix"] = by_columns(ycat_ref[...], wout_ref)

    def pe():
        v["pe"] = by_columns(p_ref[...].astype(BF16), wpe_ref)

    def ln():
        r = DN_ALPHA * h_ref[...] + v.pop("mix")
        mu_r = jnp.mean(r, axis=-1, keepdims=True)
        dr = r - mu_r
        var_r = jnp.mean(dr * dr, axis=-1, keepdims=True)
        v["hn"] = dr * lax.rsqrt(var_r + LN_EPS) * lnw_ref[...] + lnb_ref[...]

    def e3():
        v["gate_pre"] = by_columns(v["hn"].astype(BF16), wpg_ref)

    def out():
        out_ref[...] = v["hn"] + _sigmoid(v["gate_pre"] + bpg_ref[...]) * v["pe"]

    stages = {"cum": cum, "cg": cg, "raw_s": raw_s, "raw_r": raw_r, "e1": e1, "pe": pe, "s_vec": s_vec,
              "s_exp": s_exp, "ln": ln, "e3": e3, "s_fin": s_fin, "r_inter": r_inter, "r_upd": r_upd,
              "out": out, "g_prep": g_prep, "g_rec": g_rec, "g_norm": g_norm, "r_norm": r_norm}
    for hd in range(SSD_HEADS):
        stages[f"s_m{hd}"] = functools.partial(s_head, hd)
        stages[f"s_d{hd}"] = functools.partial(s_dot, hd)
    for g in range(2):
        stages[f"s_st{g}"] = functools.partial(s_state, g)
    for hh in range(4):
        stages[f"r_y{hh}"] = functools.partial(r_y, hh)
    for c in range(nc):
        stages[f"g_s{c}"] = functools.partial(g_score, c)
        stages[f"g_a{c}"] = functools.partial(g_contrib, c)
        stages[f"g_i{c}"] = functools.partial(g_inter, c)
    _run_stages(stages, MIX_ORDER)


def _mix_call(acts, dec, sbs, sbg, sbr, h, p, layer, wout, wpg, wpe, bpg, lnw, lnb,
              snw, gnw, rnw, rnb, dsk, lg_row, lg_vec):
    B, S, D = h.shape
    T = TILE
    nT = S // T
    n_tiles = B * nT
    scan_bt = lambda s: (lax.div(jnp.minimum(s, n_tiles - 1), nT), lax.rem(jnp.minimum(s, n_tiles - 1), nT))
    fin_bt = lambda s: (lax.div(jnp.maximum(s - 1, 0), nT), lax.rem(jnp.maximum(s - 1, 0), nT))
    const = lambda shape: pl.BlockSpec(shape, lambda s: (0,) * len(shape))
    scan_tile = lambda width: pl.BlockSpec((None, T, width), lambda s: (*scan_bt(s), 0))
    fin_tile = lambda width: pl.BlockSpec((None, T, width), lambda s: (*fin_bt(s), 0))
    state = lambda width: pl.BlockSpec((None, None, 256, width), lambda s: (*scan_bt(s), 0, 0))
    in_specs = [
        scan_tile(N_ACT), scan_tile(N_DEC), state(256), state(128), state(256), fin_tile(D),
        pl.BlockSpec((None, None, T, D_PLE), lambda s: (layer, *fin_bt(s), 0)),
        const((D, D)), const((D, D)), const((D_PLE, D)), const((1, D)), const((1, D)), const((1, D)),
        const((1, SSD_WIDTH)), const((1, 256)), const((1, 256)), const((1, 256)), const((1, SSD_WIDTH)),
        const((1, 256)),
        pl.BlockSpec(memory_space=pltpu.SMEM),
    ]
    scratch = [
        pltpu.VMEM((256, 256), F32), pltpu.VMEM((256, 128), F32), pltpu.VMEM((256, 256), F32),
        pltpu.VMEM((4, T, T), F32), pltpu.VMEM((T, D), BF16),
        pltpu.VMEM((T, T), BF16), pltpu.VMEM((T, T), BF16),
        pltpu.VMEM((256, SSD_WIDTH), BF16), pltpu.VMEM((256, SSD_WIDTH), BF16),
        pltpu.VMEM((256, 256), BF16), pltpu.VMEM((3, T, 256), F32),
    ]
    return pl.pallas_call(
        functools.partial(_mix_body, tiles_per_seq=nT),
        grid=(n_tiles + 1,),
        in_specs=in_specs,
        out_specs=fin_tile(D),
        out_shape=jax.ShapeDtypeStruct((B, S, D), F32),
        scratch_shapes=scratch,
        compiler_params=pltpu.CompilerParams(
            dimension_semantics=("arbitrary",), vmem_limit_bytes=VMEM_LIMIT),
        name="mix",
    )(acts, dec, sbs, sbg, sbr, h, p, wout, wpg, wpe, bpg, lnw, lnb, snw, gnw, rnw, rnb, dsk,
      lg_row, lg_vec)


def _permute_proj(w):
    pieces = [w[:, _Z0:_DT0], w[:, _GQ0:_GA0], w[:, _RQ0:_NIN], w[:, _DT0:_GQ0], w[:, _GA0:_RQ0],
              jnp.zeros((w.shape[0], N_PROJ - _NIN), w.dtype)]
    return jnp.concatenate(pieces, axis=1).astype(BF16)


def kernel(x, p, w_in, conv_w, conv_b, dt_bias, a_log, d_skip, ssd_norm_w, gla_w_a2, gla_b_a,
           gla_norm_w, ret_norm_w, ret_norm_b, w_out, ln_w, ln_b, w_pe, w_pg, b_pg):
    B, S, D = x.shape
    assert D == D_MODEL and S % (PREP_SUB * TILE) == 0 and TILE % GLA_CHUNK == 0
    half = RET_DK // 2
    inv = ROPE_BASE ** (-jnp.arange(half, dtype=F32) / half)
    cos_t, sin_t = _rope_table(jnp.tile(inv, 256 // half)[None, :], S)
    log_gamma = jnp.log(1.0 - 2.0 ** (-5.0 - jnp.arange(4, dtype=F32)))
    lg_row = jnp.repeat(log_gamma, 64)[None, :]
    pad112 = jnp.zeros((112,), F32)
    h = x
    for i in range(DEPTH):
        w_p = _permute_proj(w_in[i])
        dtb_row = jnp.concatenate([dt_bias[i].reshape(16), pad112])[None, :]
        alog_row = jnp.concatenate([a_log[i].reshape(16), pad112])[None, :]
        wsm = jnp.zeros((128, 256), F32)
        wsm = wsm.at[16:32, 0:128].set(gla_w_a2[i, 0]).at[32:48, 128:256].set(gla_w_a2[i, 1]).astype(BF16)
        acts, dec, sbs, sbg, sbr = _prep_call(
            h, w_p, conv_w[i], conv_b[i][None, :], dtb_row, alog_row, wsm,
            gla_b_a[i].reshape(1, 256), cos_t, sin_t, lg_row)
        h = _mix_call(
            acts, dec, sbs, sbg, sbr, h, p, i,
            w_out[i].astype(BF16), w_pg[i].astype(BF16), w_pe[i].astype(BF16), b_pg[i][None, :],
            ln_w[i][None, :], ln_b[i][None, :], ssd_norm_w[i][None, :],
            jnp.tile(gla_norm_w[i], 4)[None, :], ret_norm_w[i][None, :], ret_norm_b[i][None, :],
            jnp.repeat(d_skip[i], 64)[None, :], lg_row, log_gamma)
    return h
```

```python
import functools

import jax
import jax.numpy as jnp
from jax import lax
from jax.experimental import pallas as pl
from jax.experimental.pallas import tpu as pltpu

F32 = jnp.float32
BF16 = jnp.bfloat16

D_MODEL = 1024
D_PLE = 256
DEPTH = 2
SSD_HEADS = 8
SSD_WIDTH = 512
SSD_STATE = 128
SSD_CONV = 5
SSD_CONV_CH = 1024
GLA_DK = 32
GLA_TEMP = 16.0
RET_DK = 64
ROPE_BASE = 10000.0
DN_ALPHA = float((2 * DEPTH) ** 0.25)
LN_EPS = 1e-5
RMS_EPS = 1e-6
LOG2E = 1.4426950408889634

_Z0, _XBC0, _DT0, _GQ0, _GK0, _GV0, _GG0, _GA0, _RQ0, _RK0, _RV0, _RG0, _NIN = (
    0, 512, 1536, 1552, 1680, 1808, 2064, 2320, 2352, 2608, 2864, 3120, 3376)
N_PROJ = 3456
REST0 = 1536
N_ACT = 3328
N_DEC = 512

TILE = 256
PREP_SUB = 4
HALO = 16
GLA_CHUNK = 64
VMEM_LIMIT = 56 * 1024 * 1024

PREP_ORDER = (
    "x0", "small", "x3", "x1", "c0", "c1", "x2", "rk_d", "rq_d", "z0", "c3", "c2", "rq_v", "rv_d", "rk_v",
    "gqk_d", "gv_d", "sb_out3", "st_r3", "z1", "rg_d", "st_s3", "gg_d", "st_g3", "sb_out2", "st_s2",
    "st_g2", "st_r2", "sb_out1", "st_s1", "st_g1", "st_r1", "sb_out0", "st_s0", "st_g0", "st_r0")
MIX_ORDER = (
    "cg", "cum", "raw_r", "raw_s", "e1", "pe", "s_vec", "s_exp", "s_m3", "s_d3", "s_m0", "s_d0", "s_m1",
    "s_d1", "s_m7", "s_d7", "s_m4", "s_d4", "s_m6", "s_d6", "r_y0", "r_y1", "r_y3", "s_m5", "s_d5", "ln",
    "e3", "r_inter", "r_upd", "s_m2", "s_d2", "g_prep", "g_s3", "r_y2", "g_a0", "out", "g_a3", "g_s1", "g_a1",
    "g_s2", "g_a2", "g_s0", "s_st0", "s_st1", "s_fin", "g_rec", "g_i0", "g_i1", "g_i2", "g_i3", "g_norm",
    "r_norm")


def _mm(a, b):
    return jnp.dot(a, b, preferred_element_type=F32)


def _mm_nt(a, b):
    return lax.dot_general(a, b, (((1,), (1,)), ((), ())), preferred_element_type=F32)


def _mm_tn(a, b):
    return lax.dot_general(a, b, (((0,), (0,)), ((), ())), preferred_element_type=F32)


def _iota(shape, dim):
    return lax.broadcasted_iota(jnp.int32, shape, dim)


def _split(x, parts):
    out, r = [], x
    for k in range(parts):
        piece = r.astype(BF16)
        out.append(piece)
        if k + 1 < parts:
            r = r - piece.astype(F32)
    return jnp.concatenate(out, axis=1)


def _fold(y, parts):
    w = y.shape[1] // parts
    acc = y[:, :w]
    for k in range(1, parts):
        acc = acc + y[:, k * w:(k + 1) * w]
    return acc


def _cumsum_rows(tri, x, parts=3):
    return _fold(_mm(tri, _split(x, parts)), parts)


def _head_expander(first_lane, parts):
    r = _iota((128 * parts, SSD_WIDTH), 0) & 127
    c = _iota((128 * parts, SSD_WIDTH), 1) >> 6
    return jnp.where(r == c + first_lane, 1.0, 0.0).astype(BF16)


def _expand(blk, expander):
    return _mm(_split(blk, 2), expander)


def _sigmoid(x):
    return 1.0 / (1.0 + jnp.exp(-x))


def _softplus(x):
    return jnp.maximum(x, 0.0) + jnp.log1p(jnp.exp(-jnp.abs(x)))


def _tile_rows(x, n):
    return jnp.concatenate([x] * n, axis=0)


def _run_stages(stages, order):
    assert sorted(order) == sorted(stages), "every stage exactly once"
    for name in order:
        stages[name]()


def _rope_body(inv_ref, cos_ref, sin_ref):
    rows = cos_ref.shape[0]
    pos = (_iota((rows, 256), 0) + pl.program_id(0) * rows).astype(F32)
    ang = pos * inv_ref[...]
    first_half = (_iota((rows, 256), 1) & 63) < 32
    s = jnp.sin(ang)
    cos_ref[...] = jnp.cos(ang)
    sin_ref[...] = jnp.where(first_half, -s, s)


def _rope_table(inv_row, seq):
    rows = 512
    return pl.pallas_call(
        _rope_body,
        grid=(seq // rows,),
        in_specs=[pl.BlockSpec((1, 256), lambda i: (0, 0))],
        out_specs=[pl.BlockSpec((rows, 256), lambda i: (i, 0))] * 2,
        out_shape=[jax.ShapeDtypeStruct((seq, 256), F32)] * 2,
        name="rope_table",
    )(inv_row)


def _prep_body(hp_ref, h_ref, hn_ref, w_ref, cw_ref, cb_ref, dtb_ref, alog_ref, wsm_ref, ba_ref,
               cos_ref, sin_ref, lg_ref,
               acts_ref, dec_ref, sbs_ref, sbg_ref, sbr_ref,
               st_s, st_g, st_r, tri_ref, expb_ref, rtab_ref):
    T = TILE
    P = PREP_SUB * TILE
    i = pl.program_id(1)
    n = pl.num_programs(1)
    t = n - 1 - i

    @pl.when(i == 0)
    def _():
        st_s[...] = jnp.zeros_like(st_s)
        st_g[...] = jnp.zeros_like(st_g)
        st_r[...] = jnp.zeros_like(st_r)

    @pl.when((pl.program_id(0) == 0) & (i == 0))
    def _():
        tri_ref[...] = jnp.where(_iota((T, T), 0) >= _iota((T, T), 1), 1.0, 0.0).astype(BF16)
        expb_ref[...] = _head_expander(SSD_HEADS, 2)
        rtab_ref[...] = jnp.exp(_iota((T, 256), 0).astype(F32) * lg_ref[...])

    hp = jnp.where(t > 0, hp_ref[...], 0.0)
    hn = jnp.where(t < n - 1, hn_ref[...], 0.0)
    hext = jnp.concatenate([hp, h_ref[...], hn], axis=0).astype(BF16)
    hm = hext[HALO:HALO + P]
    first_half = (_iota((P, 256), 1) & 63) < 32
    sub_rows = [slice(T * s, T * s + T) for s in range(PREP_SUB)]
    v = {}

    def proj(lhs, first, width=256):
        return _mm(lhs, w_ref[:, first:first + width])

    def rotary(u):
        swapped = jnp.where(first_half, pltpu.roll(u, 224, 1), pltpu.roll(u, 32, 1))
        return u * cos_ref[...] + swapped * sin_ref[...]

    def small():
        sm = proj(hm, REST0 + 1792, 128)
        lane128 = _iota((1, 128), 1)
        dt = _softplus(sm + dtb_ref[...])
        la_s = dt * jnp.where(lane128 < 2 * SSD_HEADS, -jnp.exp(alog_ref[...]), 0.0)
        pre = _mm(sm.astype(BF16), wsm_ref[...]) + ba_ref[...]
        la_g = -_softplus(-pre) * (1.0 / GLA_TEMP)
        dec_ref[:, 0:128] = dt
        dec_ref[:, 128:256] = la_s
        dec_ref[:, 256:512] = la_g
        lab = jnp.concatenate([la_s, la_g[:, 128:256]], axis=1)
        for s, rows in enumerate(sub_rows):
            cum = _cumsum_rows(tri_ref[...], lab[rows])
            v["dt", s] = dt[rows]
            v["cbx", s] = cum - lab[rows]
            v["tot", s] = cum[T - 1:T, :]

    def x_piece(c):
        v["xs", c] = proj(hext, _XBC0 + 256 * c)

    def conv_piece(c):
        cols = slice(256 * c, 256 * c + 256)
        xs = v.pop(("xs", c))
        acc = jnp.broadcast_to(cb_ref[:, cols], (P, 256))
        for j in range(SSD_CONV):
            r0 = HALO - SSD_CONV // 2 + j
            acc = acc + cw_ref[j:j + 1, cols] * xs[r0:r0 + P, :]
        v["xbc", c] = acc * _sigmoid(acc)
        acts_ref[:, cols] = v["xbc", c].astype(BF16)

    def silu(u):
        return u * _sigmoid(u)

    def z_piece(c):
        acts_ref[:, 1024 + 256 * c:1280 + 256 * c] = silu(proj(hm, 256 * c)).astype(BF16)

    def rq_d():
        v["rq"] = proj(hm, REST0 + 768)

    def rq_v():
        acts_ref[:, 2304:2560] = rotary(v.pop("rq")).astype(BF16)

    def rk_d():
        v["rk_raw"] = proj(hm, REST0 + 1024)

    def rk_v():
        v["rk"] = rotary(v.pop("rk_raw")) * (RET_DK ** -0.5)
        acts_ref[:, 2560:2816] = v["rk"].astype(BF16)

    def rv_d():
        v["rv"] = proj(hm, REST0 + 1280).astype(BF16)
        acts_ref[:, 2816:3072] = v["rv"]

    def gqk_d():
        gqk = proj(hm, REST0)
        v["gk"] = gqk[:, 128:256]
        acts_ref[:, 1536:1664] = (gqk[:, 0:128] * (GLA_DK ** -0.5)).astype(BF16)
        acts_ref[:, 1664:1792] = v["gk"].astype(BF16)

    def gv_d():
        v["gv"] = proj(hm, REST0 + 256).astype(BF16)
        acts_ref[:, 1792:2048] = v["gv"]

    def gg_d():
        acts_ref[:, 2048:2304] = silu(proj(hm, REST0 + 512)).astype(BF16)

    def rg_d():
        acts_ref[:, 3072:3328] = silu(proj(hm, REST0 + 1536)).astype(BF16)

    def sb_out(s):
        sbs_ref[s] = st_s[...]
        sbg_ref[s] = st_g[...]
        sbr_ref[s] = st_r[...]

    def st_s_update(s):
        tile = sub_rows[s]
        exp_b = expb_ref[...]
        wx = _expand(jnp.exp(v["cbx", s][:, 0:128]) * v["dt", s], exp_b)
        dec_row = _expand(jnp.broadcast_to(jnp.exp(v["tot", s][:, 0:128]), (8, 128)), exp_b)[0:1]
        bm = v["xbc", 2][tile].astype(BF16)
        for g in range(2):
            rows = slice(128 * g, 128 * g + 128)
            cols = slice(256 * g, 256 * g + 256)
            xw = (v["xbc", g][tile] * wx[:, cols]).astype(BF16)
            st_s[rows, :] = st_s[rows, :] * dec_row[:, cols] + _mm_tn(bm[:, rows], xw)

    def st_g_update(s):
        tile = sub_rows[s]
        mask_g = (_iota((256, 128), 0) >> 6) == (_iota((256, 128), 1) >> 5)
        kw = (v["gk"][tile] * jnp.exp(v["cbx", s][:, 128:256])).astype(BF16)
        st_g[...] = (st_g[...] * jnp.exp(v["tot", s][:, 128:256])
                     + jnp.where(mask_g, _mm_tn(v["gv"][tile], kw), 0.0))

    def st_r_update(s):
        tile = sub_rows[s]
        mask_r = (_iota((256, 256), 0) >> 6) == (_iota((256, 256), 1) >> 6)
        kwr = (v["rk"][tile] * rtab_ref[...]).astype(BF16)
        st_r[...] = (st_r[...] * jnp.exp(float(T) * lg_ref[...])
                     + jnp.where(mask_r, _mm_tn(v["rv"][tile], kwr), 0.0))

    stages = {"small": small, "rq_d": rq_d, "rq_v": rq_v, "rk_d": rk_d, "rk_v": rk_v, "rv_d": rv_d,
              "gqk_d": gqk_d, "gv_d": gv_d, "gg_d": gg_d, "rg_d": rg_d}
    for s in range(PREP_SUB):
        stages[f"sb_out{s}"] = functools.partial(sb_out, s)
        stages[f"st_s{s}"] = functools.partial(st_s_update, s)
        stages[f"st_g{s}"] = functools.partial(st_g_update, s)
        stages[f"st_r{s}"] = functools.partial(st_r_update, s)
    for c in range(4):
        stages[f"x{c}"] = functools.partial(x_piece, c)
        stages[f"c{c}"] = functools.partial(conv_piece, c)
    for c in range(2):
        stages[f"z{c}"] = functools.partial(z_piece, c)
    _run_stages(stages, PREP_ORDER)


def _prep_call(h, w_p, cw, cb, dtb_row, alog_row, wsm, ba_row, cos_t, sin_t, lg_row):
    B, S, D = h.shape
    T = TILE
    P = PREP_SUB * TILE
    nT = S // T
    nP = S // P
    hb = P // HALO
    const = lambda shape: pl.BlockSpec(shape, lambda b, i: (0,) * len(shape))
    in_specs = [
        pl.BlockSpec((None, HALO, D), lambda b, i: (b, jnp.maximum((nP - 1 - i) * hb - 1, 0), 0)),
        pl.BlockSpec((None, P, D), lambda b, i: (b, nP - 1 - i, 0)),
        pl.BlockSpec((None, HALO, D), lambda b, i: (b, jnp.minimum((nP - i) * hb, S // HALO - 1), 0)),
        const((D, N_PROJ)), const((SSD_CONV, SSD_CONV_CH)), const((1, SSD_CONV_CH)),
        const((1, 128)), const((1, 128)), const((128, 256)), const((1, 256)),
        pl.BlockSpec((P, 256), lambda b, i: (nP - 1 - i, 0)),
        pl.BlockSpec((P, 256), lambda b, i: (nP - 1 - i, 0)),
        const((1, 256)),
    ]
    out_specs = [
        pl.BlockSpec((None, P, N_ACT), lambda b, i: (b, nP - 1 - i, 0)),
        pl.BlockSpec((None, P, N_DEC), lambda b, i: (b, nP - 1 - i, 0)),
        pl.BlockSpec((None, PREP_SUB, 256, 256), lambda b, i: (b, nP - 1 - i, 0, 0)),
        pl.BlockSpec((None, PREP_SUB, 256, 128), lambda b, i: (b, nP - 1 - i, 0, 0)),
        pl.BlockSpec((None, PREP_SUB, 256, 256), lambda b, i: (b, nP - 1 - i, 0, 0)),
    ]
    out_shape = [
        jax.ShapeDtypeStruct((B, S, N_ACT), BF16),
        jax.ShapeDtypeStruct((B, S, N_DEC), F32),
        jax.ShapeDtypeStruct((B, nT, 256, 256), F32),
        jax.ShapeDtypeStruct((B, nT, 256, 128), F32),
        jax.ShapeDtypeStruct((B, nT, 256, 256), F32),
    ]
    scratch = [pltpu.VMEM((256, 256), F32), pltpu.VMEM((256, 128), F32), pltpu.VMEM((256, 256), F32),
               pltpu.VMEM((T, T), BF16), pltpu.VMEM((256, SSD_WIDTH), BF16), pltpu.VMEM((T, 256), F32)]
    return pl.pallas_call(
        _prep_body,
        grid=(B, nP),
        in_specs=in_specs,
        out_specs=out_specs,
        out_shape=out_shape,
        scratch_shapes=scratch,
        compiler_params=pltpu.CompilerParams(
            dimension_semantics=("arbitrary", "arbitrary"), vmem_limit_bytes=VMEM_LIMIT),
        name="prep",
    )(h, h, h, w_p, cw, cb, dtb_row, alog_row, wsm, ba_row, cos_t, sin_t, lg_row)


def _group_mean(x, ones_blocks):
    return _mm(x.astype(BF16), ones_blocks) * (1.0 / 64.0)


def _mix_body(acts_ref, dec_ref, sbs_ref, sbg_ref, sbr_ref, h_ref, p_ref,
              wout_ref, wpg_ref, wpe_ref, bpg_ref, lnw_ref, lnb_ref,
              snw_ref, gnw_ref, rnw_ref, rnb_ref, dsk_ref, lg_ref, lgs_ref,
              out_ref, sf_s, sf_g, sf_r, dm_ref, ycat_ref,
              tri_ref, tric_ref, expf_ref, expb_ref, ones_ref, rtab_ref, *, tiles_per_seq):
    T = TILE
    H = T // 2
    L = GLA_CHUNK
    nc = T // L
    step = pl.program_id(0)
    scan_tile = jnp.minimum(step, pl.num_programs(0) - 2)

    @pl.when(lax.rem(scan_tile, tiles_per_seq) == 0)
    def _():
        sf_s[...] = jnp.zeros_like(sf_s)
        sf_g[...] = jnp.zeros_like(sf_g)
        sf_r[...] = jnp.zeros_like(sf_r)

    row = _iota((T, T), 0)
    col = _iota((T, T), 1)
    lower = row >= col
    mask_v = (_iota((256, 256), 0) >> 6) == (_iota((256, 256), 1) >> 6)

    @pl.when(step == 0)
    def _():
        ycat_ref[...] = jnp.zeros_like(ycat_ref)
        dist = jnp.abs(row - col).astype(F32)
        for hh in range(4):
            dm_ref[hh] = jnp.exp(dist * lgs_ref[hh])
        tri_ref[...] = jnp.where(lower, 1.0, 0.0).astype(BF16)
        tric_ref[...] = jnp.where(lower & ((row >> 6) == (col >> 6)), 1.0, 0.0).astype(BF16)
        expf_ref[...] = _head_expander(0, 2)
        expb_ref[...] = _head_expander(SSD_HEADS, 2)
        ones_ref[...] = jnp.where(mask_v, 1.0, 0.0).astype(BF16)
        tpos = _iota((T, 256), 0).astype(F32)
        rtab_ref[0] = jnp.exp((tpos + 1.0) * lg_ref[...])
        rtab_ref[1] = jnp.exp((float(T) - tpos) * lg_ref[...])
        rtab_ref[2] = jnp.exp((float(T - 1) - tpos) * lg_ref[...])

    mask_k = (_iota((256, 128), 0) >> 6) == (_iota((256, 128), 1) >> 5)
    mask_kk = (_iota((256, 256), 0) >> 6) == ((_iota((256, 256), 1) & 127) >> 5)
    lane_head = _iota((T, 256), 1) >> 6
    lower_h = lower[0:H, 0:H]
    lower_c = _iota((L, 256), 0) >= (_iota((L, 256), 1) & (L - 1))
    zero_bf = jnp.zeros((T, 256), BF16)
    group_rows = [slice(128 * g, 128 * g + 128) for g in range(2)]
    group_cols = [slice(256 * g, 256 * g + 256) for g in range(2)]
    x_bf = acts_ref[:, 0:512]
    b_bf = acts_ref[:, 512:768]
    c_bf = acts_ref[:, 768:1024]
    q_bf = acts_ref[:, 2304:2560]
    k_bf = acts_ref[:, 2560:2816]
    v_r = acts_ref[:, 2816:3072]
    v_bf = acts_ref[:, 1792:2048]
    v = {"y_g": [jnp.zeros((T, 256), F32), jnp.zeros((T, 256), F32)], "o_r": jnp.zeros((T, 256), F32),
         "y_c": [None] * nc, "a_c": [None] * nc}

    def cum():
        v["cum"] = _cumsum_rows(tri_ref[...], dec_ref[:, 128:256])

    def raw_s():
        v["raw_s"] = [_mm_nt(c_bf[:, group_rows[g]], b_bf[:, group_rows[g]]) for g in range(2)]

    def s_vec():
        cm = v["cum"]
        cbx = cm - dec_ref[:, 128:256]
        ldt = jnp.log(dec_ref[:, 0:128])
        v["col_f"] = cm * LOG2E
        v["col_b"] = cbx * LOG2E
        v["row_f"] = ((cm - ldt) * LOG2E).T
        v["row_b"] = ((cbx + ldt) * LOG2E).T

    def s_exp():
        dt = dec_ref[:, 0:128]
        cm = v["cum"]
        cbx = cm - dec_ref[:, 128:256]
        tot = cm[T - 1:T, :]
        exp_f = expf_ref[...]
        v["e_f"] = _expand(jnp.exp(cm), exp_f)
        v["e_b"] = _expand(jnp.exp(tot - cbx), expb_ref[...])
        w_f = _expand(jnp.exp(tot - cm) * dt, exp_f)
        v["dec_f"] = _expand(jnp.broadcast_to(jnp.exp(tot), (8, 128)), exp_f)[0:1]
        v["x_f"] = x_bf.astype(F32)
        v["xw"] = (v["x_f"] * w_f).astype(BF16)

    def s_head(hd):
        g, hh = divmod(hd, 4)
        hb = SSD_HEADS + hd
        cf_h, cb_h = v["col_f"][:, hd:hd + 1], v["col_b"][:, hb:hb + 1]
        rf_h, rb_h = v["row_f"][hd:hd + 1, :], v["row_b"][hb:hb + 1, :]
        top = jnp.concatenate(
            [jnp.where(lower_h, cf_h[0:H] - rf_h[:, 0:H], rb_h[:, 0:H] - cb_h[0:H]),
             rb_h[:, H:T] - cb_h[0:H]], axis=1)
        bot = jnp.concatenate(
            [cf_h[H:T] - rf_h[:, 0:H],
             jnp.where(lower_h, cf_h[H:T] - rf_h[:, H:T], rb_h[:, H:T] - cb_h[H:T])], axis=1)
        m = jnp.exp2(jnp.concatenate([top, bot], axis=0))
        v["s", hd] = (v["raw_s"][g] * m).astype(BF16)

    def s_dot(hd):
        g, hh = divmod(hd, 4)
        x_g = x_bf[:, group_cols[g]]
        v["y_g"][g] = v["y_g"][g] + _mm(v.pop(("s", hd)), jnp.where(lane_head == hh, x_g, jnp.zeros_like(x_g)))

    def s_state(g):
        rows, cols = group_rows[g], group_cols[g]
        st = jnp.concatenate([sf_s[rows, :], sbs_ref[rows, :]], axis=1).astype(BF16)
        cs = _mm(c_bf[:, rows], st)
        v["y_g"][g] = v["y_g"][g] + v["e_f"][:, cols] * cs[:, 0:256] + v["e_b"][:, cols] * cs[:, 256:512]
        sf_s[rows, :] = sf_s[rows, :] * v["dec_f"][:, cols] + _mm_tn(b_bf[:, rows], v["xw"][:, cols])

    def s_fin():
        y_ssd = jnp.concatenate(v["y_g"], axis=1) + dsk_ref[...] * v["x_f"]
        yz = y_ssd * acts_ref[:, 1024:1536].astype(F32)
        y1 = yz * lax.rsqrt(jnp.mean(yz * yz, axis=-1, keepdims=True) + RMS_EPS) * snw_ref[...]
        ycat_ref[:, 0:512] = y1.astype(BF16)

    def raw_r():
        v["raw_r"] = [_mm_nt(jnp.where(lane_head == hh, q_bf, zero_bf), k_bf) for hh in range(4)]

    def r_y(hh):
        s = (v["raw_r"][hh] * dm_ref[hh]).astype(BF16)
        v["o_r"] = v["o_r"] + _mm(s, jnp.where(lane_head == hh, v_r, zero_bf))

    def r_inter():
        q_f32 = q_bf.astype(F32)
        qq = jnp.concatenate([q_f32 * rtab_ref[0], q_f32 * rtab_ref[1]], axis=1).astype(BF16)
        ss = jnp.concatenate([sf_r[...], sbr_ref[...]], axis=1).astype(BF16)
        v["o_r"] = v["o_r"] + _mm_nt(qq, ss)

    def r_upd():
        kwf = (k_bf.astype(F32) * rtab_ref[2]).astype(BF16)
        sf_r[...] = sf_r[...] * jnp.exp(float(T) * lg_ref[...]) + jnp.where(mask_v, _mm_tn(v_r, kwf), 0.0)

    def r_norm():
        o_r = v["o_r"]
        mu = _group_mean(o_r, ones_ref[...])
        d = o_r - mu
        var = _group_mean(d * d, ones_ref[...])
        gate = acts_ref[:, 3072:3328].astype(F32)
        y3 = (d * lax.rsqrt(var + LN_EPS) * rnw_ref[...] + rnb_ref[...]) * gate
        ycat_ref[:, 768:1024] = y3.astype(BF16)

    def cg():
        v["cg"] = _cumsum_rows(tric_ref[...], dec_ref[:, 256:512], parts=2)

    def g_prep():
        cgv = v["cg"]
        cf = cgv[:, 0:128]
        cbx_g = cgv[:, 128:256] - dec_ref[:, 384:512]
        v["last"] = [cgv[L * c + L - 1:L * c + L, :] for c in range(nc)]
        last_full = jnp.concatenate([jnp.broadcast_to(r, (L, 256)) for r in v["last"]], axis=0)
        q = acts_ref[:, 1536:1664].astype(F32)
        k = acts_ref[:, 1664:1792].astype(F32)
        v["qf"] = (q * jnp.exp(cf)).astype(BF16)
        v["kf"] = (k * jnp.exp(-cf)).astype(BF16)
        v["qb"] = (q * jnp.exp(-cbx_g)).astype(BF16)
        v["kb"] = (k * jnp.exp(cbx_g)).astype(BF16)
        v["ksf"] = (k * jnp.exp(last_full[:, 0:128] - cf)).astype(BF16)
        v["qbi"] = (q * jnp.exp(last_full[:, 128:256] - cbx_g)).astype(BF16)

    def g_score(c):
        sl = slice(L * c, L * c + L)
        kf_t = jnp.where(mask_k, _tile_rows(v["kf"][sl], 4), jnp.zeros((256, 128), BF16))
        kb_t = jnp.where(mask_k, _tile_rows(v["kb"][sl], 4), jnp.zeros((256, 128), BF16))
        sc = jnp.where(lower_c, _mm_nt(v["qf"][sl], kf_t), _mm_nt(v["qb"][sl], kb_t)).astype(BF16)
        vd = jnp.where(mask_v, _tile_rows(v_bf[sl], 4), jnp.zeros((256, 256), BF16))
        v["y_c"][c] = _mm(sc, vd)

    def g_contrib(c):
        sl = slice(L * c, L * c + L)
        a = _mm_tn(v_bf[sl], jnp.concatenate([v["ksf"][sl], v["kb"][sl]], axis=1))
        v["a_c"][c] = jnp.where(mask_kk, a, 0.0)

    def g_rec():
        s_f, s_b = [None] * nc, [None] * nc
        s = sf_g[...]
        for c in range(nc):
            s_f[c] = s
            s = s * jnp.exp(v["last"][c][:, 0:128]) + v["a_c"][c][:, 0:128]
        sf_g[...] = s
        s = sbg_ref[...]
        for c in reversed(range(nc)):
            s_b[c] = s
            s = s * jnp.exp(v["last"][c][:, 128:256]) + v["a_c"][c][:, 128:256]
        v["ss_c"] = [jnp.concatenate([s_f[c], s_b[c]], axis=1).astype(BF16) for c in range(nc)]

    def g_inter(c):
        sl = slice(L * c, L * c + L)
        qq = jnp.concatenate([v["qf"][sl], v["qbi"][sl]], axis=1)
        v["y_c"][c] = v["y_c"][c] + _mm_nt(qq, v["ss_c"][c])

    def g_norm():
        o_g = jnp.concatenate(v["y_c"], axis=0)
        gg = acts_ref[:, 2048:2304].astype(F32)
        ms = _group_mean(o_g * o_g, ones_ref[...])
        y2 = o_g * lax.rsqrt(ms + RMS_EPS) * gnw_ref[...] * gg
        ycat_ref[:, 512:768] = y2.astype(BF16)

    def by_columns(lhs, w_ref):
        return jnp.concatenate([_mm(lhs, w_ref[:, c:c + 256]) for c in range(0, w_ref.shape[1], 256)], axis=1)

    def e1():
        v["mix"] = by_columns(ycat_ref[...], wout_ref)

    def pe():
        v["pe"] = by_columns(p_ref[...].astype(BF16), wpe_ref)

    def ln():
        r = DN_ALPHA * h_ref[...] + v.pop("mix")
        mu_r = jnp.mean(r, axis=-1, keepdims=True)
        dr = r - mu_r
        var_r = jnp.mean(dr * dr, axis=-1, keepdims=True)
        v["hn"] = dr * lax.rsqrt(var_r + LN_EPS) * lnw_ref[...] + lnb_ref[...]

    def e3():
        v["gate_pre"] = by_columns(v["hn"].astype(BF16), wpg_ref)

    def out():
        out_ref[...] = v["hn"] + _sigmoid(v["gate_pre"] + bpg_ref[...]) * v["pe"]

    stages = {"cum": cum, "cg": cg, "raw_s": raw_s, "raw_r": raw_r, "e1": e1, "pe": pe, "s_vec": s_vec,
              "s_exp": s_exp, "ln": ln, "e3": e3, "s_fin": s_fin, "r_inter": r_inter, "r_upd": r_upd,
              "out": out, "g_prep": g_prep, "g_rec": g_rec, "g_norm": g_norm, "r_norm": r_norm}
    for hd in range(SSD_HEADS):
        stages[f"s_m{hd}"] = functools.partial(s_head, hd)
        stages[f"s_d{hd}"] = functools.partial(s_dot, hd)
    for g in range(2):
        stages[f"s_st{g}"] = functools.partial(s_state, g)
    for hh in range(4):
        stages[f"r_y{hh}"] = functools.partial(r_y, hh)
    for c in range(nc):
        stages[f"g_s{c}"] = functools.partial(g_score, c)
        stages[f"g_a{c}"] = functools.partial(g_contrib, c)
        stages[f"g_i{c}"] = functools.partial(g_inter, c)
    _run_stages(stages, MIX_ORDER)


def _mix_call(acts, dec, sbs, sbg, sbr, h, p, layer, wout, wpg, wpe, bpg, lnw, lnb,
              snw, gnw, rnw, rnb, dsk, lg_row, lg_vec):
    B, S, D = h.shape
    T = TILE
    nT = S // T
    n_tiles = B * nT
    scan_bt = lambda s: (lax.div(jnp.minimum(s, n_tiles - 1), nT), lax.rem(jnp.minimum(s, n_tiles - 1), nT))
    fin_bt = lambda s: (lax.div(jnp.maximum(s - 1, 0), nT), lax.rem(jnp.maximum(s - 1, 0), nT))
    const = lambda shape: pl.BlockSpec(shape, lambda s: (0,) * len(shape))
    scan_tile = lambda width: pl.BlockSpec((None, T, width), lambda s: (*scan_bt(s), 0))
    fin_tile = lambda width: pl.BlockSpec((None, T, width), lambda s: (*fin_bt(s), 0))
    state = lambda width: pl.BlockSpec((None, None, 256, width), lambda s: (*scan_bt(s), 0, 0))
    in_specs = [
        scan_tile(N_ACT), scan_tile(N_DEC), state(256), state(128), state(256), fin_tile(D),
        pl.BlockSpec((None, None, T, D_PLE), lambda s: (layer, *fin_bt(s), 0)),
        const((D, D)), const((D, D)), const((D_PLE, D)), const((1, D)), const((1, D)), const((1, D)),
        const((1, SSD_WIDTH)), const((1, 256)), const((1, 256)), const((1, 256)), const((1, SSD_WIDTH)),
        const((1, 256)),
        pl.BlockSpec(memory_space=pltpu.SMEM),
    ]
    scratch = [
        pltpu.VMEM((256, 256), F32), pltpu.VMEM((256, 128), F32), pltpu.VMEM((256, 256), F32),
        pltpu.VMEM((4, T, T), F32), pltpu.VMEM((T, D), BF16),
        pltpu.VMEM((T, T), BF16), pltpu.VMEM((T, T), BF16),
        pltpu.VMEM((256, SSD_WIDTH), BF16), pltpu.VMEM((256, SSD_WIDTH), BF16),
        pltpu.VMEM((256, 256), BF16), pltpu.VMEM((3, T, 256), F32),
    ]
    return pl.pallas_call(
        functools.partial(_mix_body, tiles_per_seq=nT),
        grid=(n_tiles + 1,),
        in_specs=in_specs,
        out_specs=fin_tile(D),
        out_shape=jax.ShapeDtypeStruct((B, S, D), F32),
        scratch_shapes=scratch,
        compiler_params=pltpu.CompilerParams(
            dimension_semantics=("arbitrary",), vmem_limit_bytes=VMEM_LIMIT),
        name="mix",
    )(acts, dec, sbs, sbg, sbr, h, p, wout, wpg, wpe, bpg, lnw, lnb, snw, gnw, rnw, rnb, dsk,
      lg_row, lg_vec)


def _permute_proj(w):
    pieces = [w[:, _Z0:_DT0], w[:, _GQ0:_GA0], w[:, _RQ0:_NIN], w[:, _DT0:_GQ0], w[:, _GA0:_RQ0],
              jnp.zeros((w.shape[0], N_PROJ - _NIN), w.dtype)]
    return jnp.concatenate(pieces, axis=1).astype(BF16)


def kernel(x, p, w_in, conv_w, conv_b, dt_bias, a_log, d_skip, ssd_norm_w, gla_w_a2, gla_b_a,
           gla_norm_w, ret_norm_w, ret_norm_b, w_out, ln_w, ln_b, w_pe, w_pg, b_pg):
    B, S, D = x.shape
    assert D == D_MODEL and S % (PREP_SUB * TILE) == 0 and TILE % GLA_CHUNK == 0
    half = RET_DK // 2
    inv = ROPE_BASE ** (-jnp.arange(half, dtype=F32) / half)
    cos_t, sin_t = _rope_table(jnp.tile(inv, 256 // half)[None, :], S)
    log_gamma = jnp.log(1.0 - 2.0 ** (-5.0 - jnp.arange(4, dtype=F32)))
    lg_row = jnp.repeat(log_gamma, 64)[None, :]
    pad112 = jnp.zeros((112,), F32)
    h = x
    for i in range(DEPTH):
        w_p = _permute_proj(w_in[i])
        dtb_row = jnp.concatenate([dt_bias[i].reshape(16), pad112])[None, :]
        alog_row = jnp.concatenate([a_log[i].reshape(16), pad112])[None, :]
        wsm = jnp.zeros((128, 256), F32)
        wsm = wsm.at[16:32, 0:128].set(gla_w_a2[i, 0]).at[32:48, 128:256].set(gla_w_a2[i, 1]).astype(BF16)
        acts, dec, sbs, sbg, sbr = _prep_call(
            h, w_p, conv_w[i], conv_b[i][None, :], dtb_row, alog_row, wsm,
            gla_b_a[i].reshape(1, 256), cos_t, sin_t, lg_row)
        h = _mix_call(
            acts, dec, sbs, sbg, sbr, h, p, i,
            w_out[i].astype(BF16), w_pg[i].astype(BF16), w_pe[i].astype(BF16), b_pg[i][None, :],
            ln_w[i][None, :], ln_b[i][None, :], ssd_norm_w[i][None, :],
            jnp.tile(gla_norm_w[i], 4)[None, :], ret_norm_w[i][None, :], ret_norm_b[i][None, :],
            jnp.repeat(d_skip[i], 64)[None, :], lg_row, log_gamma)
    return h
```

```python
import functools

import jax
import jax.numpy as jnp
from jax import lax
from jax.experimental import pallas as pl
from jax.experimental.pallas import tpu as pltpu

F32 = jnp.float32
BF16 = jnp.bfloat16

D_MODEL = 1024
D_PLE = 256
DEPTH = 2
SSD_HEADS = 8
SSD_WIDTH = 512
SSD_STATE = 128
SSD_CONV = 5
SSD_CONV_CH = 1024
GLA_DK = 32
GLA_TEMP = 16.0
RET_DK = 64
ROPE_BASE = 10000.0
DN_ALPHA = float((2 * DEPTH) ** 0.25)
LN_EPS = 1e-5
RMS_EPS = 1e-6
LOG2E = 1.4426950408889634

_Z0, _XBC0, _DT0, _GQ0, _GK0, _GV0, _GG0, _GA0, _RQ0, _RK0, _RV0, _RG0, _NIN = (
    0, 512, 1536, 1552, 1680, 1808, 2064, 2320, 2352, 2608, 2864, 3120, 3376)
N_PROJ = 3456
REST0 = 1536
N_ACT = 3328
N_DEC = 512

TILE = 256
PREP_SUB = 4
HALO = 16
GLA_CHUNK = 64
VMEM_LIMIT = 56 * 1024 * 1024

PREP_ORDER = (
    "x0", "small", "x3", "x1", "c0", "c1", "x2", "rk_d", "rq_d", "z0", "c3", "c2", "rq_v", "rv_d", "rk_v",
    "gqk_d", "gv_d", "sb_out3", "st_r3", "z1", "rg_d", "st_s3", "gg_d", "st_g3", "sb_out2", "st_s2",
    "st_g2", "st_r2", "sb_out1", "st_s1", "st_g1", "st_r1", "sb_out0", "st_s0", "st_g0", "st_r0")
MIX_ORDER = (
    "cg", "cum", "raw_r", "raw_s", "e1", "pe", "s_vec", "s_exp", "s_m3", "s_d3", "s_m0", "s_d0", "s_m1",
    "s_d1", "s_m7", "s_d7", "s_m4", "s_d4", "s_m6", "s_d6", "r_y0", "r_y1", "r_y3", "s_m5", "s_d5", "ln",
    "e3_0", "e3_1", "out_0", "r_inter", "e3_2", "out_1", "r_upd", "e3_3", "out_2", "s_m2", "s_d2", "g_prep",
    "g_s3", "r_y2", "g_a0", "out_3", "g_a3", "g_s1", "g_a1",
    "g_s2", "g_a2", "g_s0", "s_st0", "s_st1", "s_fin", "g_rec", "g_i0", "g_i1", "g_i2", "g_i3", "g_norm",
    "r_norm")


def _mm(a, b):
    return jnp.dot(a, b, preferred_element_type=F32)


def _mm_nt(a, b):
    return lax.dot_general(a, b, (((1,), (1,)), ((), ())), preferred_element_type=F32)


def _mm_tn(a, b):
    return lax.dot_general(a, b, (((0,), (0,)), ((), ())), preferred_element_type=F32)


def _iota(shape, dim):
    return lax.broadcasted_iota(jnp.int32, shape, dim)


def _split(x, parts):
    out, r = [], x
    for k in range(parts):
        piece = r.astype(BF16)
        out.append(piece)
        if k + 1 < parts:
            r = r - piece.astype(F32)
    return jnp.concatenate(out, axis=1)


def _fold(y, parts):
    w = y.shape[1] // parts
    acc = y[:, :w]
    for k in range(1, parts):
        acc = acc + y[:, k * w:(k + 1) * w]
    return acc


def _cumsum_rows(tri, x, parts=3):
    return _fold(_mm(tri, _split(x, parts)), parts)


def _head_expander(first_lane, parts):
    r = _iota((128 * parts, SSD_WIDTH), 0) & 127
    c = _iota((128 * parts, SSD_WIDTH), 1) >> 6
    return jnp.where(r == c + first_lane, 1.0, 0.0).astype(BF16)


def _expand(blk, expander):
    return _mm(_split(blk, 2), expander)


def _sigmoid(x):
    return 1.0 / (1.0 + jnp.exp(-x))


def _softplus(x):
    return jnp.maximum(x, 0.0) + jnp.log1p(jnp.exp(-jnp.abs(x)))


def _tile_rows(x, n):
    return jnp.concatenate([x] * n, axis=0)


def _run_stages(stages, order):
    assert sorted(order) == sorted(stages), "every stage exactly once"
    for name in order:
        stages[name]()


def _rope_body(inv_ref, cos_ref, sin_ref):
    rows = cos_ref.shape[0]
    pos = (_iota((rows, 256), 0) + pl.program_id(0) * rows).astype(F32)
    ang = pos * inv_ref[...]
    first_half = (_iota((rows, 256), 1) & 63) < 32
    s = jnp.sin(ang)
    cos_ref[...] = jnp.cos(ang)
    sin_ref[...] = jnp.where(first_half, -s, s)


def _rope_table(inv_row, seq):
    rows = 512
    return pl.pallas_call(
        _rope_body,
        grid=(seq // rows,),
        in_specs=[pl.BlockSpec((1, 256), lambda i: (0, 0))],
        out_specs=[pl.BlockSpec((rows, 256), lambda i: (i, 0))] * 2,
        out_shape=[jax.ShapeDtypeStruct((seq, 256), F32)] * 2,
        name="rope_table",
    )(inv_row)


def _prep_body(hp_ref, h_ref, hn_ref, w_ref, cw_ref, cb_ref, dtb_ref, alog_ref, wsm_ref, ba_ref,
               cos_ref, sin_ref, lg_ref,
               acts_ref, dec_ref, sbs_ref, sbg_ref, sbr_ref,
               st_s, st_g, st_r, tri_ref, expb_ref, rtab_ref):
    T = TILE
    P = PREP_SUB * TILE
    i = pl.program_id(1)
    n = pl.num_programs(1)
    t = n - 1 - i

    @pl.when(i == 0)
    def _():
        st_s[...] = jnp.zeros_like(st_s)
        st_g[...] = jnp.zeros_like(st_g)
        st_r[...] = jnp.zeros_like(st_r)

    @pl.when((pl.program_id(0) == 0) & (i == 0))
    def _():
        tri_ref[...] = jnp.where(_iota((T, T), 0) >= _iota((T, T), 1), 1.0, 0.0).astype(BF16)
        expb_ref[...] = _head_expander(SSD_HEADS, 2)
        rtab_ref[...] = jnp.exp(_iota((T, 256), 0).astype(F32) * lg_ref[...])

    hp = jnp.where(t > 0, hp_ref[...], 0.0)
    hn = jnp.where(t < n - 1, hn_ref[...], 0.0)
    hext = jnp.concatenate([hp, h_ref[...], hn], axis=0).astype(BF16)
    hm = hext[HALO:HALO + P]
    first_half = (_iota((P, 256), 1) & 63) < 32
    sub_rows = [slice(T * s, T * s + T) for s in range(PREP_SUB)]
    v = {}

    def proj(lhs, first, width=256):
        return _mm(lhs, w_ref[:, first:first + width])

    def rotary(u):
        swapped = jnp.where(first_half, pltpu.roll(u, 224, 1), pltpu.roll(u, 32, 1))
        return u * cos_ref[...] + swapped * sin_ref[...]

    def small():
        sm = proj(hm, REST0 + 1792, 128)
        lane128 = _iota((1, 128), 1)
        dt = _softplus(sm + dtb_ref[...])
        la_s = dt * jnp.where(lane128 < 2 * SSD_HEADS, -jnp.exp(alog_ref[...]), 0.0)
        pre = _mm(sm.astype(BF16), wsm_ref[...]) + ba_ref[...]
        la_g = -_softplus(-pre) * (1.0 / GLA_TEMP)
        dec_ref[:, 0:128] = dt
        dec_ref[:, 128:256] = la_s
        dec_ref[:, 256:512] = la_g
        lab = jnp.concatenate([la_s, la_g[:, 128:256]], axis=1)
        for s, rows in enumerate(sub_rows):
            cum = _cumsum_rows(tri_ref[...], lab[rows])
            v["dt", s] = dt[rows]
            v["cbx", s] = cum - lab[rows]
            v["tot", s] = cum[T - 1:T, :]

    def x_piece(c):
        v["xs", c] = proj(hext, _XBC0 + 256 * c)

    def conv_piece(c):
        cols = slice(256 * c, 256 * c + 256)
        xs = v.pop(("xs", c))
        acc = jnp.broadcast_to(cb_ref[:, cols], (P, 256))
        for j in range(SSD_CONV):
            r0 = HALO - SSD_CONV // 2 + j
            acc = acc + cw_ref[j:j + 1, cols] * xs[r0:r0 + P, :]
        v["xbc", c] = acc * _sigmoid(acc)
        acts_ref[:, cols] = v["xbc", c].astype(BF16)

    def silu(u):
        return u * _sigmoid(u)

    def z_piece(c):
        acts_ref[:, 1024 + 256 * c:1280 + 256 * c] = silu(proj(hm, 256 * c)).astype(BF16)

    def rq_d():
        v["rq"] = proj(hm, REST0 + 768)

    def rq_v():
        acts_ref[:, 2304:2560] = rotary(v.pop("rq")).astype(BF16)

    def rk_d():
        v["rk_raw"] = proj(hm, REST0 + 1024)

    def rk_v():
        v["rk"] = rotary(v.pop("rk_raw")) * (RET_DK ** -0.5)
        acts_ref[:, 2560:2816] = v["rk"].astype(BF16)

    def rv_d():
        v["rv"] = proj(hm, REST0 + 1280).astype(BF16)
        acts_ref[:, 2816:3072] = v["rv"]

    def gqk_d():
        gqk = proj(hm, REST0)
        v["gk"] = gqk[:, 128:256]
        acts_ref[:, 1536:1664] = (gqk[:, 0:128] * (GLA_DK ** -0.5)).astype(BF16)
        acts_ref[:, 1664:1792] = v["gk"].astype(BF16)

    def gv_d():
        v["gv"] = proj(hm, REST0 + 256).astype(BF16)
        acts_ref[:, 1792:2048] = v["gv"]

    def gg_d():
        acts_ref[:, 2048:2304] = silu(proj(hm, REST0 + 512)).astype(BF16)

    def rg_d():
        acts_ref[:, 3072:3328] = silu(proj(hm, REST0 + 1536)).astype(BF16)

    def sb_out(s):
        sbs_ref[s] = st_s[...]
        sbg_ref[s] = st_g[...]
        sbr_ref[s] = st_r[...]

    def st_s_update(s):
        tile = sub_rows[s]
        exp_b = expb_ref[...]
        wx = _expand(jnp.exp(v["cbx", s][:, 0:128]) * v["dt", s], exp_b)
        dec_row = _expand(jnp.broadcast_to(jnp.exp(v["tot", s][:, 0:128]), (8, 128)), exp_b)[0:1]
        bm = v["xbc", 2][tile].astype(BF16)
        for g in range(2):
            rows = slice(128 * g, 128 * g + 128)
            cols = slice(256 * g, 256 * g + 256)
            xw = (v["xbc", g][tile] * wx[:, cols]).astype(BF16)
            st_s[rows, :] = st_s[rows, :] * dec_row[:, cols] + _mm_tn(bm[:, rows], xw)

    def st_g_update(s):
        tile = sub_rows[s]
        mask_g = (_iota((256, 128), 0) >> 6) == (_iota((256, 128), 1) >> 5)
        kw = (v["gk"][tile] * jnp.exp(v["cbx", s][:, 128:256])).astype(BF16)
        st_g[...] = (st_g[...] * jnp.exp(v["tot", s][:, 128:256])
                     + jnp.where(mask_g, _mm_tn(v["gv"][tile], kw), 0.0))

    def st_r_update(s):
        tile = sub_rows[s]
        mask_r = (_iota((256, 256), 0) >> 6) == (_iota((256, 256), 1) >> 6)
        kwr = (v["rk"][tile] * rtab_ref[...]).astype(BF16)
        st_r[...] = (st_r[...] * jnp.exp(float(T) * lg_ref[...])
                     + jnp.where(mask_r, _mm_tn(v["rv"][tile], kwr), 0.0))

    stages = {"small": small, "rq_d": rq_d, "rq_v": rq_v, "rk_d": rk_d, "rk_v": rk_v, "rv_d": rv_d,
              "gqk_d": gqk_d, "gv_d": gv_d, "gg_d": gg_d, "rg_d": rg_d}
    for s in range(PREP_SUB):
        stages[f"sb_out{s}"] = functools.partial(sb_out, s)
        stages[f"st_s{s}"] = functools.partial(st_s_update, s)
        stages[f"st_g{s}"] = functools.partial(st_g_update, s)
        stages[f"st_r{s}"] = functools.partial(st_r_update, s)
    for c in range(4):
        stages[f"x{c}"] = functools.partial(x_piece, c)
        stages[f"c{c}"] = functools.partial(conv_piece, c)
    for c in range(2):
        stages[f"z{c}"] = functools.partial(z_piece, c)
    _run_stages(stages, PREP_ORDER)


def _prep_call(h, w_p, cw, cb, dtb_row, alog_row, wsm, ba_row, cos_t, sin_t, lg_row):
    B, S, D = h.shape
    T = TILE
    P = PREP_SUB * TILE
    nT = S // T
    nP = S // P
    hb = P // HALO
    const = lambda shape: pl.BlockSpec(shape, lambda b, i: (0,) * len(shape))
    in_specs = [
        pl.BlockSpec((None, HALO, D), lambda b, i: (b, jnp.maximum((nP - 1 - i) * hb - 1, 0), 0)),
        pl.BlockSpec((None, P, D), lambda b, i: (b, nP - 1 - i, 0)),
        pl.BlockSpec((None, HALO, D), lambda b, i: (b, jnp.minimum((nP - i) * hb, S // HALO - 1), 0)),
        const((D, N_PROJ)), const((SSD_CONV, SSD_CONV_CH)), const((1, SSD_CONV_CH)),
        const((1, 128)), const((1, 128)), const((128, 256)), const((1, 256)),
        pl.BlockSpec((P, 256), lambda b, i: (nP - 1 - i, 0)),
        pl.BlockSpec((P, 256), lambda b, i: (nP - 1 - i, 0)),
        const((1, 256)),
    ]
    out_specs = [
        pl.BlockSpec((None, P, N_ACT), lambda b, i: (b, nP - 1 - i, 0)),
        pl.BlockSpec((None, P, N_DEC), lambda b, i: (b, nP - 1 - i, 0)),
        pl.BlockSpec((None, PREP_SUB, 256, 256), lambda b, i: (b, nP - 1 - i, 0, 0)),
        pl.BlockSpec((None, PREP_SUB, 256, 128), lambda b, i: (b, nP - 1 - i, 0, 0)),
        pl.BlockSpec((None, PREP_SUB, 256, 256), lambda b, i: (b, nP - 1 - i, 0, 0)),
    ]
    out_shape = [
        jax.ShapeDtypeStruct((B, S, N_ACT), BF16),
        jax.ShapeDtypeStruct((B, S, N_DEC), F32),
        jax.ShapeDtypeStruct((B, nT, 256, 256), F32),
        jax.ShapeDtypeStruct((B, nT, 256, 128), F32),
        jax.ShapeDtypeStruct((B, nT, 256, 256), F32),
    ]
    scratch = [pltpu.VMEM((256, 256), F32), pltpu.VMEM((256, 128), F32), pltpu.VMEM((256, 256), F32),
               pltpu.VMEM((T, T), BF16), pltpu.VMEM((256, SSD_WIDTH), BF16), pltpu.VMEM((T, 256), F32)]
    return pl.pallas_call(
        _prep_body,
        grid=(B, nP),
        in_specs=in_specs,
        out_specs=out_specs,
        out_shape=out_shape,
        scratch_shapes=scratch,
        compiler_params=pltpu.CompilerParams(
            dimension_semantics=("arbitrary", "arbitrary"), vmem_limit_bytes=VMEM_LIMIT),
        name="prep",
    )(h, h, h, w_p, cw, cb, dtb_row, alog_row, wsm, ba_row, cos_t, sin_t, lg_row)


def _group_mean(x, ones_blocks):
    return _mm(x.astype(BF16), ones_blocks) * (1.0 / 64.0)


def _mix_body(acts_ref, dec_ref, sbs_ref, sbg_ref, sbr_ref, h_ref, p_ref,
              wout_ref, wpg_ref, wpe_ref, bpg_ref, lnw_ref, lnb_ref,
              snw_ref, gnw_ref, rnw_ref, rnb_ref, dsk_ref, lg_ref, lgs_ref,
              out_ref, sf_s, sf_g, sf_r, dm_ref, ycat_ref,
              tri_ref, tric_ref, expf_ref, expb_ref, ones_ref, rtab_ref, *, tiles_per_seq):
    T = TILE
    H = T // 2
    L = GLA_CHUNK
    nc = T // L
    step = pl.program_id(0)
    scan_tile = jnp.minimum(step, pl.num_programs(0) - 2)

    @pl.when(lax.rem(scan_tile, tiles_per_seq) == 0)
    def _():
        sf_s[...] = jnp.zeros_like(sf_s)
        sf_g[...] = jnp.zeros_like(sf_g)
        sf_r[...] = jnp.zeros_like(sf_r)

    row = _iota((T, T), 0)
    col = _iota((T, T), 1)
    lower = row >= col
    mask_v = (_iota((256, 256), 0) >> 6) == (_iota((256, 256), 1) >> 6)

    @pl.when(step == 0)
    def _():
        ycat_ref[...] = jnp.zeros_like(ycat_ref)
        dist = jnp.abs(row - col).astype(F32)
        for hh in range(4):
            dm_ref[hh] = jnp.exp(dist * lgs_ref[hh])
        tri_ref[...] = jnp.where(lower, 1.0, 0.0).astype(BF16)
        tric_ref[...] = jnp.where(lower & ((row >> 6) == (col >> 6)), 1.0, 0.0).astype(BF16)
        expf_ref[...] = _head_expander(0, 2)
        expb_ref[...] = _head_expander(SSD_HEADS, 2)
        ones_ref[...] = jnp.where(mask_v, 1.0, 0.0).astype(BF16)
        tpos = _iota((T, 256), 0).astype(F32)
        rtab_ref[0] = jnp.exp((tpos + 1.0) * lg_ref[...])
        rtab_ref[1] = jnp.exp((float(T) - tpos) * lg_ref[...])
        rtab_ref[2] = jnp.exp((float(T - 1) - tpos) * lg_ref[...])

    mask_k = (_iota((256, 128), 0) >> 6) == (_iota((256, 128), 1) >> 5)
    mask_kk = (_iota((256, 256), 0) >> 6) == ((_iota((256, 256), 1) & 127) >> 5)
    lane_head = _iota((T, 256), 1) >> 6
    lower_h = lower[0:H, 0:H]
    lower_c = _iota((L, 256), 0) >= (_iota((L, 256), 1) & (L - 1))
    zero_bf = jnp.zeros((T, 256), BF16)
    group_rows = [slice(128 * g, 128 * g + 128) for g in range(2)]
    group_cols = [slice(256 * g, 256 * g + 256) for g in range(2)]
    x_bf = acts_ref[:, 0:512]
    b_bf = acts_ref[:, 512:768]
    c_bf = acts_ref[:, 768:1024]
    q_bf = acts_ref[:, 2304:2560]
    k_bf = acts_ref[:, 2560:2816]
    v_r = acts_ref[:, 2816:3072]
    v_bf = acts_ref[:, 1792:2048]
    v = {"y_g": [jnp.zeros((T, 256), F32), jnp.zeros((T, 256), F32)], "o_r": jnp.zeros((T, 256), F32),
         "y_c": [None] * nc, "a_c": [None] * nc}

    def cum():
        v["cum"] = _cumsum_rows(tri_ref[...], dec_ref[:, 128:256])

    def raw_s():
        v["raw_s"] = [_mm_nt(c_bf[:, group_rows[g]], b_bf[:, group_rows[g]]) for g in range(2)]

    def s_vec():
        cm = v["cum"]
        cbx = cm - dec_ref[:, 128:256]
        ldt = jnp.log(dec_ref[:, 0:128])
        v["col_f"] = cm * LOG2E
        v["col_b"] = cbx * LOG2E
        v["row_f"] = ((cm - ldt) * LOG2E).T
        v["row_b"] = ((cbx + ldt) * LOG2E).T

    def s_exp():
        dt = dec_ref[:, 0:128]
        cm = v["cum"]
        cbx = cm - dec_ref[:, 128:256]
        tot = cm[T - 1:T, :]
        exp_f = expf_ref[...]
        v["e_f"] = _expand(jnp.exp(cm), exp_f)
        v["e_b"] = _expand(jnp.exp(tot - cbx), expb_ref[...])
        w_f = _expand(jnp.exp(tot - cm) * dt, exp_f)
        v["dec_f"] = _expand(jnp.broadcast_to(jnp.exp(tot), (8, 128)), exp_f)[0:1]
        v["x_f"] = x_bf.astype(F32)
        v["xw"] = (v["x_f"] * w_f).astype(BF16)

    def s_head(hd):
        g, hh = divmod(hd, 4)
        hb = SSD_HEADS + hd
        cf_h, cb_h = v["col_f"][:, hd:hd + 1], v["col_b"][:, hb:hb + 1]
        rf_h, rb_h = v["row_f"][hd:hd + 1, :], v["row_b"][hb:hb + 1, :]
        top = jnp.concatenate(
            [jnp.where(lower_h, cf_h[0:H] - rf_h[:, 0:H], rb_h[:, 0:H] - cb_h[0:H]),
             rb_h[:, H:T] - cb_h[0:H]], axis=1)
        bot = jnp.concatenate(
            [cf_h[H:T] - rf_h[:, 0:H],
             jnp.where(lower_h, cf_h[H:T] - rf_h[:, H:T], rb_h[:, H:T] - cb_h[H:T])], axis=1)
        m = jnp.exp2(jnp.concatenate([top, bot], axis=0))
        v["s", hd] = (v["raw_s"][g] * m).astype(BF16)

    def s_dot(hd):
        g, hh = divmod(hd, 4)
        x_g = x_bf[:, group_cols[g]]
        v["y_g"][g] = v["y_g"][g] + _mm(v.pop(("s", hd)), jnp.where(lane_head == hh, x_g, jnp.zeros_like(x_g)))

    def s_state(g):
        rows, cols = group_rows[g], group_cols[g]
        st = jnp.concatenate([sf_s[rows, :], sbs_ref[rows, :]], axis=1).astype(BF16)
        cs = _mm(c_bf[:, rows], st)
        v["y_g"][g] = v["y_g"][g] + v["e_f"][:, cols] * cs[:, 0:256] + v["e_b"][:, cols] * cs[:, 256:512]
        sf_s[rows, :] = sf_s[rows, :] * v["dec_f"][:, cols] + _mm_tn(b_bf[:, rows], v["xw"][:, cols])

    def s_fin():
        y_ssd = jnp.concatenate(v["y_g"], axis=1) + dsk_ref[...] * v["x_f"]
        yz = y_ssd * acts_ref[:, 1024:1536].astype(F32)
        y1 = yz * lax.rsqrt(jnp.mean(yz * yz, axis=-1, keepdims=True) + RMS_EPS) * snw_ref[...]
        ycat_ref[:, 0:512] = y1.astype(BF16)

    def raw_r():
        v["raw_r"] = [_mm_nt(jnp.where(lane_head == hh, q_bf, zero_bf), k_bf) for hh in range(4)]

    def r_y(hh):
        s = (v["raw_r"][hh] * dm_ref[hh]).astype(BF16)
        v["o_r"] = v["o_r"] + _mm(s, jnp.where(lane_head == hh, v_r, zero_bf))

    def r_inter():
        q_f32 = q_bf.astype(F32)
        qq = jnp.concatenate([q_f32 * rtab_ref[0], q_f32 * rtab_ref[1]], axis=1).astype(BF16)
        ss = jnp.concatenate([sf_r[...], sbr_ref[...]], axis=1).astype(BF16)
        v["o_r"] = v["o_r"] + _mm_nt(qq, ss)

    def r_upd():
        kwf = (k_bf.astype(F32) * rtab_ref[2]).astype(BF16)
        sf_r[...] = sf_r[...] * jnp.exp(float(T) * lg_ref[...]) + jnp.where(mask_v, _mm_tn(v_r, kwf), 0.0)

    def r_norm():
        o_r = v["o_r"]
        mu = _group_mean(o_r, ones_ref[...])
        d = o_r - mu
        var = _group_mean(d * d, ones_ref[...])
        gate = acts_ref[:, 3072:3328].astype(F32)
        y3 = (d * lax.rsqrt(var + LN_EPS) * rnw_ref[...] + rnb_ref[...]) * gate
        ycat_ref[:, 768:1024] = y3.astype(BF16)

    def cg():
        v["cg"] = _cumsum_rows(tric_ref[...], dec_ref[:, 256:512], parts=2)

    def g_prep():
        cgv = v["cg"]
        cf = cgv[:, 0:128]
        cbx_g = cgv[:, 128:256] - dec_ref[:, 384:512]
        v["last"] = [cgv[L * c + L - 1:L * c + L, :] for c in range(nc)]
        last_full = jnp.concatenate([jnp.broadcast_to(r, (L, 256)) for r in v["last"]], axis=0)
        q = acts_ref[:, 1536:1664].astype(F32)
        k = acts_ref[:, 1664:1792].astype(F32)
        v["qf"] = (q * jnp.exp(cf)).astype(BF16)
        v["kf"] = (k * jnp.exp(-cf)).astype(BF16)
        v["qb"] = (q * jnp.exp(-cbx_g)).astype(BF16)
        v["kb"] = (k * jnp.exp(cbx_g)).astype(BF16)
        v["ksf"] = (k * jnp.exp(last_full[:, 0:128] - cf)).astype(BF16)
        v["qbi"] = (q * jnp.exp(last_full[:, 128:256] - cbx_g)).astype(BF16)

    def g_score(c):
        sl = slice(L * c, L * c + L)
        kf_t = jnp.where(mask_k, _tile_rows(v["kf"][sl], 4), jnp.zeros((256, 128), BF16))
        kb_t = jnp.where(mask_k, _tile_rows(v["kb"][sl], 4), jnp.zeros((256, 128), BF16))
        sc = jnp.where(lower_c, _mm_nt(v["qf"][sl], kf_t), _mm_nt(v["qb"][sl], kb_t)).astype(BF16)
        vd = jnp.where(mask_v, _tile_rows(v_bf[sl], 4), jnp.zeros((256, 256), BF16))
        v["y_c"][c] = _mm(sc, vd)

    def g_contrib(c):
        sl = slice(L * c, L * c + L)
        a = _mm_tn(v_bf[sl], jnp.concatenate([v["ksf"][sl], v["kb"][sl]], axis=1))
        v["a_c"][c] = jnp.where(mask_kk, a, 0.0)

    def g_rec():
        s_f, s_b = [None] * nc, [None] * nc
        s = sf_g[...]
        for c in range(nc):
            s_f[c] = s
            s = s * jnp.exp(v["last"][c][:, 0:128]) + v["a_c"][c][:, 0:128]
        sf_g[...] = s
        s = sbg_ref[...]
        for c in reversed(range(nc)):
            s_b[c] = s
            s = s * jnp.exp(v["last"][c][:, 128:256]) + v["a_c"][c][:, 128:256]
        v["ss_c"] = [jnp.concatenate([s_f[c], s_b[c]], axis=1).astype(BF16) for c in range(nc)]

    def g_inter(c):
        sl = slice(L * c, L * c + L)
        qq = jnp.concatenate([v["qf"][sl], v["qbi"][sl]], axis=1)
        v["y_c"][c] = v["y_c"][c] + _mm_nt(qq, v["ss_c"][c])

    def g_norm():
        o_g = jnp.concatenate(v["y_c"], axis=0)
        gg = acts_ref[:, 2048:2304].astype(F32)
        ms = _group_mean(o_g * o_g, ones_ref[...])
        y2 = o_g * lax.rsqrt(ms + RMS_EPS) * gnw_ref[...] * gg
        ycat_ref[:, 512:768] = y2.astype(BF16)

    def by_columns(lhs, w_ref):
        return jnp.concatenate([_mm(lhs, w_ref[:, c:c + 256]) for c in range(0, w_ref.shape[1], 256)], axis=1)

    def e1():
        v["mix"] = by_columns(ycat_ref[...], wout_ref)

    def pe():
        v["pe"] = by_columns(p_ref[...].astype(BF16), wpe_ref)

    def ln():
        r = DN_ALPHA * h_ref[...] + v.pop("mix")
        mu_r = jnp.mean(r, axis=-1, keepdims=True)
        dr = r - mu_r
        var_r = jnp.mean(dr * dr, axis=-1, keepdims=True)
        v["hn"] = dr * lax.rsqrt(var_r + LN_EPS) * lnw_ref[...] + lnb_ref[...]

    def e3(k):
        cols = slice(256 * k, 256 * k + 256)
        if k == 0:
            v["hn_bf"] = v["hn"].astype(BF16)
        v["gate_pre", k] = _mm(v["hn_bf"], wpg_ref[:, cols])

    def out(k):
        cols = slice(256 * k, 256 * k + 256)
        gate = _sigmoid(v.pop(("gate_pre", k)) + bpg_ref[:, cols])
        out_ref[:, cols] = v["hn"][:, cols] + gate * v["pe"][:, cols]

    stages = {"cum": cum, "cg": cg, "raw_s": raw_s, "raw_r": raw_r, "e1": e1, "pe": pe, "s_vec": s_vec,
              "s_exp": s_exp, "ln": ln, "s_fin": s_fin, "r_inter": r_inter, "r_upd": r_upd,
              "g_prep": g_prep, "g_rec": g_rec, "g_norm": g_norm, "r_norm": r_norm}
    for k in range(4):
        stages[f"e3_{k}"] = functools.partial(e3, k)
        stages[f"out_{k}"] = functools.partial(out, k)
    for hd in range(SSD_HEADS):
        stages[f"s_m{hd}"] = functools.partial(s_head, hd)
        stages[f"s_d{hd}"] = functools.partial(s_dot, hd)
    for g in range(2):
        stages[f"s_st{g}"] = functools.partial(s_state, g)
    for hh in range(4):
        stages[f"r_y{hh}"] = functools.partial(r_y, hh)
    for c in range(nc):
        stages[f"g_s{c}"] = functools.partial(g_score, c)
        stages[f"g_a{c}"] = functools.partial(g_contrib, c)
        stages[f"g_i{c}"] = functools.partial(g_inter, c)
    _run_stages(stages, MIX_ORDER)


def _mix_call(acts, dec, sbs, sbg, sbr, h, p, layer, wout, wpg, wpe, bpg, lnw, lnb,
              snw, gnw, rnw, rnb, dsk, lg_row, lg_vec):
    B, S, D = h.shape
    T = TILE
    nT = S // T
    n_tiles = B * nT
    scan_bt = lambda s: (lax.div(jnp.minimum(s, n_tiles - 1), nT), lax.rem(jnp.minimum(s, n_tiles - 1), nT))
    fin_bt = lambda s: (lax.div(jnp.maximum(s - 1, 0), nT), lax.rem(jnp.maximum(s - 1, 0), nT))
    const = lambda shape: pl.BlockSpec(shape, lambda s: (0,) * len(shape))
    scan_tile = lambda width: pl.BlockSpec((None, T, width), lambda s: (*scan_bt(s), 0))
    fin_tile = lambda width: pl.BlockSpec((None, T, width), lambda s: (*fin_bt(s), 0))
    state = lambda width: pl.BlockSpec((None, None, 256, width), lambda s: (*scan_bt(s), 0, 0))
    in_specs = [
        scan_tile(N_ACT), scan_tile(N_DEC), state(256), state(128), state(256), fin_tile(D),
        pl.BlockSpec((None, None, T, D_PLE), lambda s: (layer, *fin_bt(s), 0)),
        const((D, D)), const((D, D)), const((D_PLE, D)), const((1, D)), const((1, D)), const((1, D)),
        const((1, SSD_WIDTH)), const((1, 256)), const((1, 256)), const((1, 256)), const((1, SSD_WIDTH)),
        const((1, 256)),
        pl.BlockSpec(memory_space=pltpu.SMEM),
    ]
    scratch = [
        pltpu.VMEM((256, 256), F32), pltpu.VMEM((256, 128), F32), pltpu.VMEM((256, 256), F32),
        pltpu.VMEM((4, T, T), F32), pltpu.VMEM((T, D), BF16),
        pltpu.VMEM((T, T), BF16), pltpu.VMEM((T, T), BF16),
        pltpu.VMEM((256, SSD_WIDTH), BF16), pltpu.VMEM((256, SSD_WIDTH), BF16),
        pltpu.VMEM((256, 256), BF16), pltpu.VMEM((3, T, 256), F32),
    ]
    return pl.pallas_call(
        functools.partial(_mix_body, tiles_per_seq=nT),
        grid=(n_tiles + 1,),
        in_specs=in_specs,
        out_specs=fin_tile(D),
        out_shape=jax.ShapeDtypeStruct((B, S, D), F32),
        scratch_shapes=scratch,
        compiler_params=pltpu.CompilerParams(
            dimension_semantics=("arbitrary",), vmem_limit_bytes=VMEM_LIMIT),
        name="mix",
    )(acts, dec, sbs, sbg, sbr, h, p, wout, wpg, wpe, bpg, lnw, lnb, snw, gnw, rnw, rnb, dsk,
      lg_row, lg_vec)


def _permute_proj(w):
    pieces = [w[:, _Z0:_DT0], w[:, _GQ0:_GA0], w[:, _RQ0:_NIN], w[:, _DT0:_GQ0], w[:, _GA0:_RQ0],
              jnp.zeros((w.shape[0], N_PROJ - _NIN), w.dtype)]
    return jnp.concatenate(pieces, axis=1).astype(BF16)


def kernel(x, p, w_in, conv_w, conv_b, dt_bias, a_log, d_skip, ssd_norm_w, gla_w_a2, gla_b_a,
           gla_norm_w, ret_norm_w, ret_norm_b, w_out, ln_w, ln_b, w_pe, w_pg, b_pg):
    B, S, D = x.shape
    assert D == D_MODEL and S % (PREP_SUB * TILE) == 0 and TILE % GLA_CHUNK == 0
    half = RET_DK // 2
    inv = ROPE_BASE ** (-jnp.arange(half, dtype=F32) / half)
    cos_t, sin_t = _rope_table(jnp.tile(inv, 256 // half)[None, :], S)
    log_gamma = jnp.log(1.0 - 2.0 ** (-5.0 - jnp.arange(4, dtype=F32)))
    lg_row = jnp.repeat(log_gamma, 64)[None, :]
    pad112 = jnp.zeros((112,), F32)
    h = x
    for i in range(DEPTH):
        w_p = _permute_proj(w_in[i])
        dtb_row = jnp.concatenate([dt_bias[i].reshape(16), pad112])[None, :]
        alog_row = jnp.concatenate([a_log[i].reshape(16), pad112])[None, :]
        wsm = jnp.zeros((128, 256), F32)
        wsm = wsm.at[16:32, 0:128].set(gla_w_a2[i, 0]).at[32:48, 128:256].set(gla_w_a2[i, 1]).astype(BF16)
        acts, dec, sbs, sbg, sbr = _prep_call(
            h, w_p, conv_w[i], conv_b[i][None, :], dtb_row, alog_row, wsm,
            gla_b_a[i].reshape(1, 256), cos_t, sin_t, lg_row)
        h = _mix_call(
            acts, dec, sbs, sbg, sbr, h, p, i,
            w_out[i].astype(BF16), w_pg[i].astype(BF16), w_pe[i].astype(BF16), b_pg[i][None, :],
            ln_w[i][None, :], ln_b[i][None, :], ssd_norm_w[i][None, :],
            jnp.tile(gla_norm_w[i], 4)[None, :], ret_norm_w[i][None, :], ret_norm_b[i][None, :],
            jnp.repeat(d_skip[i], 64)[None, :], lg_row, log_gamma)
    return h
```
